```python
import jax, jax.numpy as jnp
from jax import lax
import numpy as np


D_MODEL = 1024
BATCH = 16
SEQ = 2048
DEPTH = 2

GRID_W = 64
CTX_LEN = 256
N_MIXERS = 2
HEAD_DIM = 64
N_Q_HEADS = D_MODEL // HEAD_DIM
N_KV_HEADS = N_Q_HEADS // 4
GQA_GROUP = N_Q_HEADS // N_KV_HEADS
Q_DIM = N_Q_HEADS * HEAD_DIM
KV_DIM = N_KV_HEADS * HEAD_DIM
WINDOW = 128
ATTN_BLOCK = 128
ROPE_THETA = 10000.0
ROPE_AXIS_DIM = HEAD_DIM // 2
CONV_WIDTH = 3
D_FF_DENSE = 2816
N_EXPERTS = 8
TOP_K = 2
D_FF_EXPERT = 3584
MOE_BLOCK = 128
LN_EPS = 1e-5
MASK_VALUE = -1e30
DEEPNORM_ALPHA = (2 * DEPTH) ** 0.25
DEEPNORM_BETA = (8 * DEPTH) ** -0.25
N_ATTN_LAYERS = (DEPTH + N_MIXERS - 1) // N_MIXERS
N_CONV_LAYERS = DEPTH // N_MIXERS
N_DENSE_LAYERS = (DEPTH + 1) // 2
N_MOE_LAYERS = DEPTH // 2

kernel_name = 'hybrid_window_gqa_shortconv_moe_dit'


def layer_norm(x, g, b):
    xf = x.astype(jnp.float32)
    mu = xf.mean(-1, keepdims=True)
    var = jnp.square(xf - mu).mean(-1, keepdims=True)
    return ((xf - mu) * lax.rsqrt(var + LN_EPS) * g.astype(jnp.float32) + b.astype(jnp.float32)).astype(x.dtype)


def axial_rope_tables(rows_count, dtype):
    rows = jnp.repeat(jnp.arange(rows_count, dtype=jnp.float32), GRID_W)
    cols = jnp.tile(jnp.arange(GRID_W, dtype=jnp.float32), rows_count)
    inv_freq = ROPE_THETA ** (-jnp.arange(0, ROPE_AXIS_DIM, 2, dtype=jnp.float32) / ROPE_AXIS_DIM)
    ar = rows[:, None] * inv_freq
    ac = cols[:, None] * inv_freq
    ang = jnp.concatenate([ar, ar, ac, ac], axis=-1)
    return jnp.cos(ang).astype(dtype), jnp.sin(ang).astype(dtype)


def apply_rope(x, cos, sin):
    xr = x.reshape(x.shape[:-1] + (2, 2, ROPE_AXIS_DIM // 2))
    rot = jnp.stack([-xr[..., 1, :], xr[..., 0, :]], axis=-2).reshape(x.shape)
    return x * cos[:, None, :] + rot * sin[:, None, :]


def sink_softmax(logits, sink):
    m = jnp.maximum(logits.max(-1, keepdims=True), sink)
    p = jnp.exp(logits - m)
    return p / (p.sum(-1, keepdims=True) + jnp.exp(sink - m))


def windowed_latent_attention(q, k, v, k_ctx, v_ctx, sink):
    B, S = q.shape[:2]
    n_ctx = k_ctx.shape[1]
    n_blocks = S // ATTN_BLOCK
    span = ATTN_BLOCK + 2 * WINDOW
    pad = ((0, 0), (WINDOW, WINDOW), (0, 0), (0, 0))
    k_pad = jnp.pad(k, pad)
    v_pad = jnp.pad(v, pad)
    scale = HEAD_DIM ** -0.5

    def block(i):
        s0 = i * ATTN_BLOCK
        qb = lax.dynamic_slice_in_dim(q, s0, ATTN_BLOCK, axis=1)
        kb = lax.dynamic_slice_in_dim(k_pad, s0, span, axis=1)
        vb = lax.dynamic_slice_in_dim(v_pad, s0, span, axis=1)
        q_pos = s0 + jnp.arange(ATTN_BLOCK)
        k_pos = s0 - WINDOW + jnp.arange(span)
        valid = (jnp.abs(q_pos[:, None] - k_pos[None, :]) <= WINDOW) & (k_pos >= 0) & (k_pos < S)
        s_win = jnp.einsum('bqkgd,bnkd->bkgqn', qb, kb).astype(jnp.float32) * scale
        s_win = jnp.where(valid, s_win, MASK_VALUE)
        s_ctx = jnp.einsum('bqkgd,bckd->bkgqc', qb, k_ctx).astype(jnp.float32) * scale
        probs = sink_softmax(jnp.concatenate([s_ctx, s_win], axis=-1), sink)
        o = (jnp.einsum('bkgqc,bckd->bqkgd', probs[..., :n_ctx], v_ctx.astype(jnp.float32))
             + jnp.einsum('bkgqn,bnkd->bqkgd', probs[..., n_ctx:], vb.astype(jnp.float32)))
        return o.astype(q.dtype)

    out = lax.map(block, jnp.arange(n_blocks))
    return jnp.moveaxis(out, 0, 1).reshape(B, S, Q_DIM)


def context_attention(q, k, v, sink):
    B, C = q.shape[:2]
    s = jnp.einsum('bqkgd,bckd->bkgqc', q, k).astype(jnp.float32) * HEAD_DIM ** -0.5
    o = jnp.einsum('bkgqc,bckd->bqkgd', sink_softmax(s, sink), v.astype(jnp.float32))
    return o.astype(q.dtype).reshape(B, C, Q_DIM)


def attention_mixer(h, hc, w_qkv, w_o, sinks, cos, sin, update_ctx):
    B, S, _ = h.shape
    C = hc.shape[1]
    q, k, v = jnp.split(h @ w_qkv, [Q_DIM, Q_DIM + KV_DIM], axis=-1)
    q = apply_rope(q.reshape(B, S, N_Q_HEADS, HEAD_DIM), cos, sin).reshape(B, S, N_KV_HEADS, GQA_GROUP, HEAD_DIM)
    k = apply_rope(k.reshape(B, S, N_KV_HEADS, HEAD_DIM), cos, sin)
    v = v.reshape(B, S, N_KV_HEADS, HEAD_DIM)
    if update_ctx:
        q_c, k_c, v_c = jnp.split(hc @ w_qkv, [Q_DIM, Q_DIM + KV_DIM], axis=-1)
    else:
        k_c, v_c = jnp.split(hc @ w_qkv[:, Q_DIM:], [KV_DIM], axis=-1)
    k_c = k_c.reshape(B, C, N_KV_HEADS, HEAD_DIM)
    v_c = v_c.reshape(B, C, N_KV_HEADS, HEAD_DIM)
    sink = sinks.astype(jnp.float32).reshape(N_KV_HEADS, GQA_GROUP, 1, 1)
    out = windowed_latent_attention(q, k, v, k_c, v_c, sink) @ w_o
    out_c = None
    if update_ctx:
        q_c = q_c.reshape(B, C, N_KV_HEADS, GQA_GROUP, HEAD_DIM)
        out_c = context_attention(q_c, k_c, v_c, sink) @ w_o
    return out, out_c


def short_conv_mixer(h, w_in, conv_w, w_out):
    S = h.shape[1]
    b_gate, c_gate, val = jnp.split(h @ w_in, 3, axis=-1)
    u = c_gate * val
    half = CONV_WIDTH // 2
    u_pad = jnp.pad(u, ((0, 0), (half, half), (0, 0)))
    y = sum(conv_w[j] * u_pad[:, j:j + S] for j in range(CONV_WIDTH))
    return (b_gate * y) @ w_out


def swiglu(h, w1, w3, w2):
    return (jax.nn.silu(h @ w1) * (h @ w3)) @ w2


def moe_swiglu(h, w_router, w1, w3, w2):
    T, D = h.shape
    logits = (h @ w_router).astype(jnp.float32)
    top_logit, top_e = lax.top_k(logits, TOP_K)
    gates = jax.nn.softmax(top_logit, axis=-1)
    e_flat = top_e.reshape(-1).astype(jnp.int32)
    tok_flat = jnp.repeat(jnp.arange(T, dtype=jnp.int32), TOP_K)
    g_flat = gates.reshape(-1)
    order = jnp.argsort(e_flat)
    e_s, tok_s, g_s = e_flat[order], tok_flat[order], g_flat[order]
    counts = jnp.zeros((N_EXPERTS,), jnp.int32).at[e_flat].add(1)
    starts = jnp.cumsum(counts) - counts
    padded = (counts + MOE_BLOCK - 1) // MOE_BLOCK * MOE_BLOCK
    pends = jnp.cumsum(padded)
    pstarts = pends - padded
    dest = pstarts[e_s] + jnp.arange(T * TOP_K, dtype=jnp.int32) - starts[e_s]
    n_blocks = -(-(T * TOP_K) // MOE_BLOCK) + N_EXPERTS
    n_rows = n_blocks * MOE_BLOCK
    row_tok = jnp.full((n_rows,), T, jnp.int32).at[dest].set(tok_s)
    row_gate = jnp.zeros((n_rows,), jnp.float32).at[dest].set(g_s)
    block_e = jnp.minimum(jnp.searchsorted(pends, jnp.arange(n_blocks, dtype=jnp.int32) * MOE_BLOCK, side='right'), N_EXPERTS - 1)
    h_pad = jnp.concatenate([h, jnp.zeros((1, D), h.dtype)], axis=0)
    xb = h_pad[row_tok].reshape(n_blocks, MOE_BLOCK, D)

    def expert_block(args):
        xblk, e = args
        return swiglu(xblk, w1[e], w3[e], w2[e])

    yb = lax.map(expert_block, (xb, block_e)).reshape(n_rows, D)
    y = jax.ops.segment_sum(yb * row_gate[:, None], row_tok, num_segments=T + 1)
    return y[:T].astype(h.dtype)


def setup_inputs(seed: int = 0) -> dict:
    key = jax.random.key(seed)
    ks = jax.random.split(key, 22)
    D = D_MODEL

    def nrm(k, shape, scale):
        return jax.random.normal(k, shape, jnp.float32) * scale

    return {
        'x': nrm(ks[0], (BATCH, SEQ, D), 1.0),
        'c': nrm(ks[1], (BATCH, D), 1.0),
        'ctx': nrm(ks[2], (BATCH, CTX_LEN, D), 1.0),
        'c_ctx': nrm(ks[3], (D,), 1.0),
        'w_mod': nrm(ks[4], (DEPTH, D, 6 * D), 0.5 * D ** -0.5),
        'b_mod': nrm(ks[5], (DEPTH, 6 * D), 0.01),
        'ln_g': 1.0 + nrm(ks[6], (DEPTH, 2, D), 0.02),
        'ln_b': nrm(ks[7], (DEPTH, 2, D), 0.02),
        'attn_w_qkv': nrm(ks[8], (N_ATTN_LAYERS, D, Q_DIM + 2 * KV_DIM), D ** -0.5),
        'attn_w_o': nrm(ks[9], (N_ATTN_LAYERS, Q_DIM, D), Q_DIM ** -0.5 * DEEPNORM_BETA),
        'attn_sink': nrm(ks[10], (N_ATTN_LAYERS, N_Q_HEADS), 0.5),
        'conv_w_in': nrm(ks[11], (N_CONV_LAYERS, D, 3 * D), D ** -0.5),
        'conv_w': nrm(ks[12], (N_CONV_LAYERS, CONV_WIDTH, D), CONV_WIDTH ** -0.5),
        'conv_w_out': nrm(ks[13], (N_CONV_LAYERS, D, D), D ** -0.5 * DEEPNORM_BETA),
        'ffn_w1': nrm(ks[14], (N_DENSE_LAYERS, D, D_FF_DENSE), D ** -0.5),
        'ffn_w3': nrm(ks[15], (N_DENSE_LAYERS, D, D_FF_DENSE), D ** -0.5),
        'ffn_w2': nrm(ks[16], (N_DENSE_LAYERS, D_FF_DENSE, D), D_FF_DENSE ** -0.5 * DEEPNORM_BETA),
        'moe_router': nrm(ks[17], (N_MOE_LAYERS, D, N_EXPERTS), D ** -0.5),
        'moe_w1': nrm(ks[18], (N_MOE_LAYERS, N_EXPERTS, D, D_FF_EXPERT), D ** -0.5),
        'moe_w3': nrm(ks[19], (N_MOE_LAYERS, N_EXPERTS, D, D_FF_EXPERT), D ** -0.5),
        'moe_w2': nrm(ks[20], (N_MOE_LAYERS, N_EXPERTS, D_FF_EXPERT, D), D_FF_EXPERT ** -0.5 * DEEPNORM_BETA),
    }


def reference(x, c, ctx, c_ctx, w_mod, b_mod, ln_g, ln_b, attn_w_qkv, attn_w_o, attn_sink,
              conv_w_in, conv_w, conv_w_out, ffn_w1, ffn_w3, ffn_w2,
              moe_router, moe_w1, moe_w3, moe_w2):
    B, S, D = x.shape
    ROWS = S // GRID_W
    cos, sin = axial_rope_tables(ROWS, x.dtype)
    silu_c = jax.nn.silu(c)
    silu_cc = jax.nn.silu(c_ctx)
    xc = ctx
    for i in range(DEPTH):
        mi = i // N_MIXERS
        fi = i // 2
        is_attn = i % N_MIXERS == 0
        is_dense = i % 2 == 0
        update_ctx = any(j % N_MIXERS == 0 for j in range(i + 1, DEPTH))

        sh1, sc1, g1, sh2, sc2, g2 = jnp.split((silu_c @ w_mod[i] + b_mod[i])[:, None, :], 6, axis=-1)
        h = x * (1 + sc1) + sh1
        if is_attn or update_ctx:
            csh1, csc1, cg1, csh2, csc2, cg2 = jnp.split(silu_cc @ w_mod[i] + b_mod[i], 6)
            hc = xc * (1 + csc1) + csh1

        if is_attn:
            mix, mix_c = attention_mixer(h, hc, attn_w_qkv[mi], attn_w_o[mi], attn_sink[mi], cos, sin, update_ctx)
        else:
            mix = short_conv_mixer(h, conv_w_in[mi], conv_w[mi], conv_w_out[mi])
            mix_c = short_conv_mixer(hc, conv_w_in[mi], conv_w[mi], conv_w_out[mi]) if update_ctx else None
        x = layer_norm(DEEPNORM_ALPHA * x + g1 * mix, ln_g[i, 0], ln_b[i, 0])
        if update_ctx:
            xc = layer_norm(DEEPNORM_ALPHA * xc + cg1 * mix_c, ln_g[i, 0], ln_b[i, 0])

        h = x * (1 + sc2) + sh2
        if update_ctx:
            hc = xc * (1 + csc2) + csh2
        if is_dense:
            ff = swiglu(h, ffn_w1[fi], ffn_w3[fi], ffn_w2[fi])
            ff_c = swiglu(hc, ffn_w1[fi], ffn_w3[fi], ffn_w2[fi]) if update_ctx else None
        else:
            tokens = h.reshape(-1, D)
            n_c = 0
            if update_ctx:
                n_c = hc.shape[0] * hc.shape[1]
                tokens = jnp.concatenate([hc.reshape(-1, D), tokens], axis=0)
            y_all = moe_swiglu(tokens, moe_router[fi], moe_w1[fi], moe_w3[fi], moe_w2[fi])
            ff = y_all[n_c:].reshape(h.shape)
            ff_c = y_all[:n_c].reshape(hc.shape) if update_ctx else None
        x = layer_norm(DEEPNORM_ALPHA * x + g2 * ff, ln_g[i, 1], ln_b[i, 1])
        if update_ctx:
            xc = layer_norm(DEEPNORM_ALPHA * xc + cg2 * ff_c, ln_g[i, 1], ln_b[i, 1])
    return x
```

```python
import functools

import jax
import jax.numpy as jnp
from jax import lax
from jax.experimental import pallas as pl
from jax.experimental.pallas import tpu as pltpu

F32 = jnp.float32
BF16 = jnp.bfloat16

GRID_W = 64
HEAD_DIM = 64
GQA_GROUP = 4
WINDOW = 128
ROPE_THETA = 10000.0
CONV_WIDTH = 3
N_EXPERTS = 8
LN_EPS = 1e-5
MASK_VALUE = -1e30
DEPTH = 2
DEEPNORM_ALPHA = (2 * DEPTH) ** 0.25

LANES = 128
MOD_ROWS = 8
VMEM_LIMIT = 56 * 1024 * 1024


def _cparams(sem):
    return pltpu.CompilerParams(dimension_semantics=sem, vmem_limit_bytes=VMEM_LIMIT)


def _const_spec(shape):
    nd = len(shape)
    return pl.BlockSpec(shape, lambda *_: (0,) * nd, pipeline_mode=pl.Buffered(1))


def _silu(a):
    return a / (1.0 + jnp.exp(-a))


def _deepnorm_ln(x, mix, gate, g, b):
    y = DEEPNORM_ALPHA * x + gate * mix
    mu = jnp.mean(y, axis=-1, keepdims=True)
    d = y - mu
    var = jnp.mean(d * d, axis=-1, keepdims=True)
    return d * lax.rsqrt(var + LN_EPS) * g + b


def _mod_kernel(c_ref, w_ref, b_ref, o_ref):
    cv = c_ref[...]
    s = _silu(cv).astype(BF16)
    o_ref[0] = jnp.dot(s, w_ref[0].astype(BF16), preferred_element_type=F32) + b_ref[0]


def _modulation(cvec, w_mod, b_mod):
    depth, d, n = w_mod.shape
    rows = cvec.shape[0]
    nt = 1536
    return pl.pallas_call(
        _mod_kernel,
        grid=(depth, n // nt),
        in_specs=[
            pl.BlockSpec((rows, d), lambda l, j: (0, 0)),
            pl.BlockSpec((1, d, nt), lambda l, j: (l, 0, j)),
            pl.BlockSpec((1, 1, nt), lambda l, j: (l, 0, j)),
        ],
        out_specs=pl.BlockSpec((1, rows, nt), lambda l, j: (l, 0, j)),
        out_shape=jax.ShapeDtypeStruct((depth, rows, n), F32),
        compiler_params=_cparams(("parallel", "parallel")),
        name="modulation",
    )(cvec, w_mod, b_mod.reshape(depth, 1, n))


def _proj_kernel(x_ref, mod_ref, w_ref, cos_ref, sin_ref, *o_refs, rope_lanes, q_lanes, splits):
    m = mod_ref[0]
    h = x_ref[...] * (1.0 + m[1:2, :]) + m[0:1, :]
    y = jnp.dot(h.astype(BF16), w_ref[...], preferred_element_type=F32)
    if rope_lanes:
        cos = cos_ref[...]
        sin = sin_ref[...]
        first_half = (lax.broadcasted_iota(jnp.int32, (1, LANES), 1) % 32) < 16
    off = 0
    for o_ref, width in zip(o_refs, splits):
        for c0 in range(0, width, LANES):
            yc = y[:, off + c0:off + c0 + LANES]
            if off + c0 < rope_lanes:
                rot = jnp.where(first_half, pltpu.roll(yc, LANES - 16, 1), pltpu.roll(yc, 16, 1))
                yc = yc * cos + rot * sin
                if off + c0 < q_lanes:
                    yc = yc * (HEAD_DIM ** -0.5)
            o_ref[:, c0:c0 + LANES] = yc.astype(o_ref.dtype)
        off += width


def _project(x, mod, w, cos, sin, *, rows_per_mod, tm, rope_lanes, q_lanes, splits, name):
    t, d = x.shape
    n = w.shape[1]
    steps_per_mod = rows_per_mod // tm
    steps_per_seq = cos.shape[0] // tm
    kern = functools.partial(_proj_kernel, rope_lanes=rope_lanes, q_lanes=q_lanes, splits=splits)
    return pl.pallas_call(
        kern,
        grid=(t // tm,),
        in_specs=[
            pl.BlockSpec((tm, d), lambda i: (i, 0)),
            pl.BlockSpec((1, MOD_ROWS, d), lambda i: (i // steps_per_mod, 0, 0)),
            _const_spec((d, n)),
            pl.BlockSpec((tm, LANES), lambda i: (i % steps_per_seq, 0)),
            pl.BlockSpec((tm, LANES), lambda i: (i % steps_per_seq, 0)),
        ],
        out_specs=[pl.BlockSpec((tm, s), lambda i: (i, 0)) for s in splits],
        out_shape=[jax.ShapeDtypeStruct((t, s), BF16) for s in splits],
        compiler_params=_cparams(("parallel",)),
        name=name,
    )(x, mod, w, cos, sin)


def _attn_kernel(sink_ref, q_ref, km_ref, kp_ref, kn_ref, vm_ref, vp_ref, vn_ref, kvc_ref, o_ref, *, tq, seq, n_ctx):
    i = pl.program_id(1)
    nsub = tq // WINDOW
    span = 3 * WINDOW
    nkv = km_ref.shape[2] // LANES
    lane = lax.broadcasted_iota(jnp.int32, (1, LANES), 1)
    lo = lane < HEAD_DIM
    r = lax.broadcasted_iota(jnp.int32, (WINDOW, n_ctx + span), 0)
    c = lax.broadcasted_iota(jnp.int32, (WINDOW, n_ctx + span), 1) - n_ctx
    band = (c >= r) & (c <= r + 2 * WINDOW)
    valids = []
    for j in range(nsub):
        kpos = c + (i * tq + (j - 1) * WINDOW)
        valids.append((c < 0) | (band & (kpos >= 0) & (kpos < seq)))
    gidx = lax.broadcasted_iota(jnp.int32, (GQA_GROUP, 1, 1), 0)
    zero = jnp.zeros((), BF16)
    for kh in range(nkv):
        ks = slice(kh * LANES, (kh + 1) * LANES)
        kall = jnp.concatenate([kp_ref[0, :, ks], km_ref[0, :, ks], kn_ref[0, :, ks]], axis=0)
        vall = jnp.concatenate([vp_ref[0, :, ks], vm_ref[0, :, ks], vn_ref[0, :, ks]], axis=0)
        kc = kvc_ref[0, :, ks]
        vc = kvc_ref[0, :, nkv * LANES + kh * LANES:nkv * LANES + (kh + 1) * LANES]
        sink = jnp.full((GQA_GROUP, 1, 1), sink_ref[kh * GQA_GROUP], F32)
        for g in range(1, GQA_GROUP):
            sink = jnp.where(gidx == g, sink_ref[kh * GQA_GROUP + g], sink)
        for j in range(nsub):
            rows = slice(j * WINDOW, (j + 1) * WINDOW)
            kk = jnp.concatenate([kc, kall[j * WINDOW:j * WINDOW + span]], axis=0)
            vv = jnp.concatenate([vc, vall[j * WINDOW:j * WINDOW + span]], axis=0)
            parts = []
            for t in range(GQA_GROUP // 2):
                q2 = q_ref[0, rows, (2 * kh + t) * LANES:(2 * kh + t + 1) * LANES]
                parts += [jnp.where(lo, q2, zero), jnp.where(lo, zero, q2)]
            lhs = jnp.concatenate(parts, axis=0)
            s = lax.dot_general(lhs, kk, (((1,), (1,)), ((), ())), preferred_element_type=F32)
            s = s.reshape(GQA_GROUP, WINDOW, n_ctx + span)
            s = jnp.where(valids[j][None], s, MASK_VALUE)
            m = jnp.maximum(jnp.max(s, axis=-1, keepdims=True), sink)
            p = jnp.exp(s - m)
            denom = jnp.sum(p, axis=-1, keepdims=True) + jnp.exp(sink - m)
            o = jnp.dot(p.astype(BF16).reshape(GQA_GROUP * WINDOW, n_ctx + span), vv, preferred_element_type=F32)
            o = o.reshape(GQA_GROUP, WINDOW, LANES) / denom
            for t in range(GQA_GROUP // 2):
                o2 = jnp.where(lo, o[2 * t], o[2 * t + 1])
                o_ref[0, rows, (2 * kh + t) * LANES:(2 * kh + t + 1) * LANES] = o2.astype(o_ref.dtype)


def _attention(q, kd, vd, kvc, sink, *, tq):
    b, s, qd = q.shape
    kvd = kd.shape[2]
    n_ctx = kvc.shape[1]
    nblk = s // WINDOW
    per = tq // WINDOW
    kern = functools.partial(_attn_kernel, tq=tq, seq=s, n_ctx=n_ctx)
    main = lambda bi, i: (bi, i, 0)
    prev = lambda bi, i: (bi, jnp.maximum(i * per - 1, 0), 0)
    nxt = lambda bi, i: (bi, jnp.minimum((i + 1) * per, nblk - 1), 0)
    return pl.pallas_call(
        kern,
        grid=(b, s // tq),
        in_specs=[
            pl.BlockSpec(memory_space=pltpu.SMEM),
            pl.BlockSpec((1, tq, qd), main),
            pl.BlockSpec((1, tq, kvd), main),
            pl.BlockSpec((1, WINDOW, kvd), prev),
            pl.BlockSpec((1, WINDOW, kvd), nxt),
            pl.BlockSpec((1, tq, kvd), main),
            pl.BlockSpec((1, WINDOW, kvd), prev),
            pl.BlockSpec((1, WINDOW, kvd), nxt),
            pl.BlockSpec((1, n_ctx, 2 * kvd), lambda bi, i: (bi, 0, 0)),
        ],
        out_specs=pl.BlockSpec((1, tq, qd), main),
        out_shape=jax.ShapeDtypeStruct((b, s, qd), BF16),
        compiler_params=_cparams(("parallel", "parallel")),
        name="window_attention",
    )(sink, q, kd, kd, kd, vd, vd, vd, kvc)


def _oproj_kernel(a_ref, w_ref, x_ref, mod_ref, g_ref, b_ref, o_ref):
    mix = jnp.dot(a_ref[...], w_ref[...], preferred_element_type=F32)
    o_ref[...] = _deepnorm_ln(x_ref[...], mix, mod_ref[0][2:3, :], g_ref[...], b_ref[...])


def _oproj_ln(a, w, x, mod, g, b, *, rows_per_mod, tm):
    t, d = x.shape
    spm = rows_per_mod // tm
    return pl.pallas_call(
        _oproj_kernel,
        grid=(t // tm,),
        in_specs=[
            pl.BlockSpec((tm, a.shape[1]), lambda i: (i, 0)),
            _const_spec(w.shape),
            pl.BlockSpec((tm, d), lambda i: (i, 0)),
            pl.BlockSpec((1, MOD_ROWS, d), lambda i: (i // spm, 0, 0)),
            _const_spec((1, d)),
            _const_spec((1, d)),
        ],
        out_specs=pl.BlockSpec((tm, d), lambda i: (i, 0)),
        out_shape=jax.ShapeDtypeStruct((t, d), F32),
        compiler_params=_cparams(("parallel",)),
        name="oproj_ln",
    )(a, w, x, mod, g, b)


def _ffn_kernel(x_ref, mod_ref, w1_ref, w3_ref, w2_ref, g_ref, b_ref, o_ref, *, fc):
    m = mod_ref[0]
    x = x_ref[...]
    hb = (x * (1.0 + m[4:5, :]) + m[3:4, :]).astype(BF16)
    acc = jnp.zeros(x.shape, F32)
    for c0 in range(0, w1_ref.shape[1], fc):
        a = jnp.dot(hb, w1_ref[:, c0:c0 + fc], preferred_element_type=F32)
        bb = jnp.dot(hb, w3_ref[:, c0:c0 + fc], preferred_element_type=F32)
        gact = (_silu(a) * bb).astype(BF16)
        acc = acc + jnp.dot(gact, w2_ref[c0:c0 + fc, :], preferred_element_type=F32)
    o_ref[...] = _deepnorm_ln(x, acc, m[5:6, :], g_ref[...], b_ref[...])


def _ffn_ln(x, mod, w1, w3, w2, g, b, *, rows_per_mod, tm, fc):
    t, d = x.shape
    spm = rows_per_mod // tm
    return pl.pallas_call(
        functools.partial(_ffn_kernel, fc=fc),
        grid=(t // tm,),
        in_specs=[
            pl.BlockSpec((tm, d), lambda i: (i, 0)),
            pl.BlockSpec((1, MOD_ROWS, d), lambda i: (i // spm, 0, 0)),
            _const_spec(w1.shape),
            _const_spec(w3.shape),
            _const_spec(w2.shape),
            _const_spec((1, d)),
            _const_spec((1, d)),
        ],
        out_specs=pl.BlockSpec((tm, d), lambda i: (i, 0)),
        out_shape=jax.ShapeDtypeStruct((t, d), F32),
        compiler_params=_cparams(("parallel",)),
        name="dense_swiglu_ln",
    )(x, mod, w1, w3, w2, g, b)


def _convin_kernel(x_ref, mod_ref, w_ref, u_ref, bg_ref):
    m = mod_ref[0]
    d = x_ref.shape[1]
    hb = (x_ref[...] * (1.0 + m[1:2, :]) + m[0:1, :]).astype(BF16)
    y = jnp.dot(hb, w_ref[...], preferred_element_type=F32)
    bg_ref[...] = y[:, :d].astype(bg_ref.dtype)
    u_ref[...] = (y[:, d:2 * d] * y[:, 2 * d:]).astype(u_ref.dtype)


def _conv_in(x, mod, w, *, rows_per_mod, tm):
    t, d = x.shape
    spm = rows_per_mod // tm
    return pl.pallas_call(
        _convin_kernel,
        grid=(t // tm,),
        in_specs=[
            pl.BlockSpec((tm, d), lambda i: (i, 0)),
            pl.BlockSpec((1, MOD_ROWS, d), lambda i: (i // spm, 0, 0)),
            _const_spec(w.shape),
        ],
        out_specs=[pl.BlockSpec((tm, d), lambda i: (i, 0))] * 2,
        out_shape=[jax.ShapeDtypeStruct((t, d), BF16)] * 2,
        compiler_params=_cparams(("parallel",)),
        name="conv_in",
    )(x, mod, w)


HALO = 16


def _convout_kernel(u_ref, up_ref, un_ref, bg_ref, cw_ref, w_ref, x_ref, mod_ref, g_ref, b_ref, wr_ref,
                    x_out, h_out, gate_out):
    j = pl.program_id(1)
    nj = pl.num_programs(1)
    m = mod_ref[0]
    u = u_ref[0].astype(F32)
    tm = u.shape[0]
    row = lax.broadcasted_iota(jnp.int32, (tm, 1), 0)
    prev_row = jnp.where(j == 0, 0.0, up_ref[0, HALO - 1:HALO, :].astype(F32))
    next_row = jnp.where(j == nj - 1, 0.0, un_ref[0, 0:1, :].astype(F32))
    u_prev = jnp.where(row == 0, prev_row, pltpu.roll(u, 1, 0))
    u_next = jnp.where(row == tm - 1, next_row, pltpu.roll(u, tm - 1, 0))
    cw = cw_ref[...]
    y = cw[0:1, :] * u_prev + cw[1:2, :] * u + cw[2:3, :] * u_next
    z = (bg_ref[0].astype(F32) * y).astype(BF16)
    mix = jnp.dot(z, w_ref[...], preferred_element_type=F32)
    x_new = _deepnorm_ln(x_ref[0], mix, m[2:3, :], g_ref[...], b_ref[...])
    x_out[0] = x_new
    hb = (x_new * (1.0 + m[4:5, :]) + m[3:4, :]).astype(BF16)
    h_out[0] = hb
    lg = jnp.dot(hb, wr_ref[...], preferred_element_type=F32)
    lane = lax.broadcasted_iota(jnp.int32, lg.shape, 1)
    neg = jnp.float32(-jnp.inf)
    lg = jnp.where(lane < N_EXPERTS, lg, neg)
    m1 = jnp.max(lg, axis=-1, keepdims=True)
    i1 = jnp.min(jnp.where(lg == m1, lane, LANES), axis=-1, keepdims=True)
    lg2 = jnp.where(lane == i1, neg, lg)
    m2 = jnp.max(lg2, axis=-1, keepdims=True)
    i2 = jnp.min(jnp.where(lg2 == m2, lane, LANES), axis=-1, keepdims=True)
    e2 = jnp.exp(m2 - m1)
    gate1 = 1.0 / (1.0 + e2)
    gate2 = e2 / (1.0 + e2)
    gate_out[0] = jnp.where(lane == i1, gate1, jnp.where(lane == i2, gate2, 0.0))


def _conv_out(u, bg, cw, w, x, mod, g, b, wr, *, tm):
    bsz, s, d = x.shape
    per = tm // HALO
    nh = s // HALO
    main = lambda bi, j: (bi, j, 0)
    return pl.pallas_call(
        _convout_kernel,
        grid=(bsz, s // tm),
        in_specs=[
            pl.BlockSpec((1, tm, d), main),
            pl.BlockSpec((1, HALO, d), lambda bi, j: (bi, jnp.maximum(j * per - 1, 0), 0)),
            pl.BlockSpec((1, HALO, d), lambda bi, j: (bi, jnp.minimum((j + 1) * per, nh - 1), 0)),
            pl.BlockSpec((1, tm, d), main),
            _const_spec(cw.shape),
            _const_spec(w.shape),
            pl.BlockSpec((1, tm, d), main),
            pl.BlockSpec((1, MOD_ROWS, d), lambda bi, j: (bi, 0, 0)),
            _const_spec((1, d)),
            _const_spec((1, d)),
            _const_spec(wr.shape),
        ],
        out_specs=[
            pl.BlockSpec((1, tm, d), main),
            pl.BlockSpec((1, tm, d), main),
            pl.BlockSpec((1, tm, LANES), main),
        ],
        out_shape=[
            jax.ShapeDtypeStruct((bsz, s, d), F32),
            jax.ShapeDtypeStruct((bsz, s, d), BF16),
            jax.ShapeDtypeStruct((bsz, s, LANES), F32),
        ],
        compiler_params=_cparams(("parallel", "parallel")),
        name="conv_out_ln_router",
    )(u, u, u, bg, cw, w, x, mod, g, b, wr)


def _moe_kernel(h_ref, gate_ref, w1_ref, w3_ref, w2_ref, x_ref, mod_ref, g_ref, b_ref, o_ref, acc_ref):
    e = pl.program_id(1)
    c = pl.program_id(2)
    last = (e == pl.num_programs(1) - 1) & (c == pl.num_programs(2) - 1)

    @pl.when((e == 0) & (c == 0))
    def _():
        acc_ref[...] = jnp.zeros_like(acc_ref)

    hb = h_ref[...]
    gates = gate_ref[...]
    lane = lax.broadcasted_iota(jnp.int32, gates.shape, 1)
    ge = jnp.sum(jnp.where(lane == e, gates, 0.0), axis=-1, keepdims=True)
    a = jnp.dot(hb, w1_ref[0], preferred_element_type=F32)
    bb = jnp.dot(hb, w3_ref[0], preferred_element_type=F32)
    gact = (_silu(a) * bb).astype(BF16)
    acc_ref[...] += ge * jnp.dot(gact, w2_ref[0], preferred_element_type=F32)

    @pl.when(last)
    def _():
        o_ref[...] = _deepnorm_ln(x_ref[...], acc_ref[...], mod_ref[0][5:6, :], g_ref[...], b_ref[...])


def _moe_ln(h, gates, w1, w3, w2, x, mod, g, b, *, rows_per_mod, tm, fc):
    t, d = x.shape
    ne, _, ff = w1.shape
    spm = rows_per_mod // tm
    return pl.pallas_call(
        _moe_kernel,
        grid=(t // tm, ne, ff // fc),
        in_specs=[
            pl.BlockSpec((tm, d), lambda i, e, c: (i, 0)),
            pl.BlockSpec((tm, LANES), lambda i, e, c: (i, 0)),
            pl.BlockSpec((1, d, fc), lambda i, e, c: (e, 0, c)),
            pl.BlockSpec((1, d, fc), lambda i, e, c: (e, 0, c)),
            pl.BlockSpec((1, fc, d), lambda i, e, c: (e, c, 0)),
            pl.BlockSpec((tm, d), lambda i, e, c: (i, 0)),
            pl.BlockSpec((1, MOD_ROWS, d), lambda i, e, c: (i // spm, 0, 0)),
            _const_spec((1, d)),
            _const_spec((1, d)),
        ],
        out_specs=pl.BlockSpec((tm, d), lambda i, e, c: (i, 0)),
        out_shape=jax.ShapeDtypeStruct((t, d), F32),
        scratch_shapes=[pltpu.VMEM((tm, d), F32)],
        compiler_params=_cparams(("parallel", "arbitrary", "arbitrary")),
        name="moe_swiglu_ln",
    )(h, gates, w1, w3, w2, x, mod, g, b)


def _rope_tables(rows_count):
    half = HEAD_DIM // 2
    rows = jnp.repeat(jnp.arange(rows_count, dtype=F32), GRID_W)
    cols = jnp.tile(jnp.arange(GRID_W, dtype=F32), rows_count)
    inv_freq = ROPE_THETA ** (-jnp.arange(0, half, 2, dtype=F32) / half)
    ar = rows[:, None] * inv_freq
    ac = cols[:, None] * inv_freq
    ang = jnp.concatenate([ar, ar, ac, ac], axis=-1)
    sign = jnp.where((jnp.arange(HEAD_DIM) % 32) < 16, -1.0, 1.0).astype(F32)
    cos = jnp.tile(jnp.cos(ang), (1, LANES // HEAD_DIM))
    sin = jnp.tile(jnp.sin(ang) * sign, (1, LANES // HEAD_DIM))
    return cos, sin


def _dup_heads(w, n_heads):
    d = w.shape[0]
    w = w.reshape(d, n_heads, 1, HEAD_DIM)
    return jnp.broadcast_to(w, (d, n_heads, 2, HEAD_DIM)).reshape(d, n_heads * 2 * HEAD_DIM)


def _mod_blocks(mod_rows, d):
    r = mod_rows.shape[0]
    m = mod_rows.reshape(r, 6, d)
    return jnp.concatenate([m, jnp.zeros((r, MOD_ROWS - 6, d), F32)], axis=1)


def kernel(x, c, ctx, c_ctx, w_mod, b_mod, ln_g, ln_b, attn_w_qkv, attn_w_o, attn_sink,
           conv_w_in, conv_w, conv_w_out, ffn_w1, ffn_w3, ffn_w2,
           moe_router, moe_w1, moe_w3, moe_w2):
    bsz, seq, d = x.shape
    n_ctx = ctx.shape[1]
    t = bsz * seq
    q_dim = attn_w_o.shape[1]
    kv_dim = (attn_w_qkv.shape[2] - q_dim) // 2
    n_kv = kv_dim // HEAD_DIM

    pad_rows = (-(bsz + 1)) % 8
    cvec = jnp.concatenate([c, c_ctx[None, :], jnp.zeros((pad_rows, d), F32)], axis=0)
    mod_all = _modulation(cvec, w_mod, b_mod)
    mod0 = _mod_blocks(mod_all[0, :bsz], d)
    modc0 = _mod_blocks(mod_all[0, bsz:bsz + 1], d)
    mod1 = _mod_blocks(mod_all[1, :bsz], d)

    cos, sin = _rope_tables(seq // GRID_W)

    wqkv = attn_w_qkv[0]
    wk = _dup_heads(wqkv[:, q_dim:q_dim + kv_dim], n_kv)
    wv = _dup_heads(wqkv[:, q_dim + kv_dim:], n_kv)
    w_all = jnp.concatenate([wqkv[:, :q_dim], wk, wv], axis=1).astype(BF16)
    w_kv = jnp.concatenate([wk, wv], axis=1).astype(BF16)
    x2d = x.reshape(t, d)
    q, kd, vd = _project(x2d, mod0, w_all, cos, sin, rows_per_mod=seq, tm=512,
                         rope_lanes=q_dim + 2 * kv_dim, q_lanes=q_dim,
                         splits=(q_dim, 2 * kv_dim, 2 * kv_dim), name="qkv_rope")
    (kvc,) = _project(ctx.reshape(bsz * n_ctx, d), modc0, w_kv, cos, sin, rows_per_mod=bsz * n_ctx, tm=512,
                      rope_lanes=0, q_lanes=0, splits=(4 * kv_dim,), name="ctx_kv")
    attn = _attention(q.reshape(bsz, seq, q_dim), kd.reshape(bsz, seq, 2 * kv_dim), vd.reshape(bsz, seq, 2 * kv_dim),
                      kvc.reshape(bsz, n_ctx, 4 * kv_dim), attn_sink[0], tq=256)
    x1 = _oproj_ln(attn.reshape(t, q_dim), attn_w_o[0].astype(BF16), x2d, mod0,
                   ln_g[0, 0][None], ln_b[0, 0][None], rows_per_mod=seq, tm=512)
    x2 = _ffn_ln(x1, mod0, ffn_w1[0].astype(BF16), ffn_w3[0].astype(BF16), ffn_w2[0].astype(BF16),
                 ln_g[0, 1][None], ln_b[0, 1][None], rows_per_mod=seq, tm=512, fc=256)

    u, bg = _conv_in(x2, mod1, conv_w_in[0].astype(BF16), rows_per_mod=seq, tm=512)
    cw = jnp.concatenate([conv_w[0], jnp.zeros((8 - CONV_WIDTH, d), F32)], axis=0)
    wr = jnp.concatenate([moe_router[0], jnp.zeros((d, LANES - N_EXPERTS), F32)], axis=1).astype(BF16)
    x3, h4, gates = _conv_out(u.reshape(bsz, seq, d), bg.reshape(bsz, seq, d), cw, conv_w_out[0].astype(BF16),
                              x2.reshape(bsz, seq, d), mod1, ln_g[1, 0][None], ln_b[1, 0][None], wr, tm=512)
    out = _moe_ln(h4.reshape(t, d), gates.reshape(t, LANES), moe_w1[0].astype(BF16), moe_w3[0].astype(BF16),
                  moe_w2[0].astype(BF16), x3.reshape(t, d), mod1, ln_g[1, 1][None], ln_b[1, 1][None],
                  rows_per_mod=seq, tm=1024, fc=512)
    return out.reshape(bsz, seq, d)
```

```python
import functools

import jax
import jax.numpy as jnp
from jax import lax
from jax.experimental import pallas as pl
from jax.experimental.pallas import tpu as pltpu

F32 = jnp.float32
BF16 = jnp.bfloat16

GRID_W = 64
HEAD_DIM = 64
GQA_GROUP = 4
WINDOW = 128
ROPE_THETA = 10000.0
CONV_WIDTH = 3
N_EXPERTS = 8
LN_EPS = 1e-5
MASK_VALUE = -1e30
DEPTH = 2
DEEPNORM_ALPHA = (2 * DEPTH) ** 0.25

LANES = 128
MOD_ROWS = 8
VMEM_LIMIT = 56 * 1024 * 1024


def _cparams(sem):
    return pltpu.CompilerParams(dimension_semantics=sem, vmem_limit_bytes=VMEM_LIMIT)


def _const_spec(shape):
    nd = len(shape)
    return pl.BlockSpec(shape, lambda *_: (0,) * nd, pipeline_mode=pl.Buffered(1))


def _silu(a):
    return a / (1.0 + jnp.exp(-a))


def _deepnorm_ln(x, mix, gate, g, b):
    y = DEEPNORM_ALPHA * x + gate * mix
    mu = jnp.mean(y, axis=-1, keepdims=True)
    d = y - mu
    var = jnp.mean(d * d, axis=-1, keepdims=True)
    return d * lax.rsqrt(var + LN_EPS) * g + b


def _mod_kernel(c_ref, w_ref, b_ref, o_ref):
    cv = c_ref[...]
    s = _silu(cv).astype(BF16)
    o_ref[0] = jnp.dot(s, w_ref[0].astype(BF16), preferred_element_type=F32) + b_ref[0]


def _modulation(cvec, w_mod, b_mod):
    depth, d, n = w_mod.shape
    rows = cvec.shape[0]
    nt = 1536
    return pl.pallas_call(
        _mod_kernel,
        grid=(depth, n // nt),
        in_specs=[
            pl.BlockSpec((rows, d), lambda l, j: (0, 0)),
            pl.BlockSpec((1, d, nt), lambda l, j: (l, 0, j)),
            pl.BlockSpec((1, 1, nt), lambda l, j: (l, 0, j)),
        ],
        out_specs=pl.BlockSpec((1, rows, nt), lambda l, j: (l, 0, j)),
        out_shape=jax.ShapeDtypeStruct((depth, rows, n), F32),
        compiler_params=_cparams(("parallel", "parallel")),
        name="modulation",
    )(cvec, w_mod, b_mod.reshape(depth, 1, n))


def _proj_kernel(x_ref, mod_ref, w_ref, cos_ref, sin_ref, *o_refs, rope_lanes, q_lanes, splits):
    m = mod_ref[0]
    h = x_ref[...] * (1.0 + m[1:2, :]) + m[0:1, :]
    y = jnp.dot(h.astype(BF16), w_ref[...], preferred_element_type=F32)
    if rope_lanes:
        cos = cos_ref[...]
        sin = sin_ref[...]
        first_half = (lax.broadcasted_iota(jnp.int32, (1, LANES), 1) % 32) < 16
    off = 0
    for o_ref, width in zip(o_refs, splits):
        for c0 in range(0, width, LANES):
            yc = y[:, off + c0:off + c0 + LANES]
            if off + c0 < rope_lanes:
                rot = jnp.where(first_half, pltpu.roll(yc, LANES - 16, 1), pltpu.roll(yc, 16, 1))
                yc = yc * cos + rot * sin
                if off + c0 < q_lanes:
                    yc = yc * (HEAD_DIM ** -0.5)
            o_ref[:, c0:c0 + LANES] = yc.astype(o_ref.dtype)
        off += width


def _project(x, mod, w, cos, sin, *, rows_per_mod, tm, rope_lanes, q_lanes, splits, name):
    t, d = x.shape
    n = w.shape[1]
    steps_per_mod = rows_per_mod // tm
    steps_per_seq = cos.shape[0] // tm
    kern = functools.partial(_proj_kernel, rope_lanes=rope_lanes, q_lanes=q_lanes, splits=splits)
    return pl.pallas_call(
        kern,
        grid=(t // tm,),
        in_specs=[
            pl.BlockSpec((tm, d), lambda i: (i, 0)),
            pl.BlockSpec((1, MOD_ROWS, d), lambda i: (i // steps_per_mod, 0, 0)),
            _const_spec((d, n)),
            pl.BlockSpec((tm, LANES), lambda i: (i % steps_per_seq, 0)),
            pl.BlockSpec((tm, LANES), lambda i: (i % steps_per_seq, 0)),
        ],
        out_specs=[pl.BlockSpec((tm, s), lambda i: (i, 0)) for s in splits],
        out_shape=[jax.ShapeDtypeStruct((t, s), BF16) for s in splits],
        compiler_params=_cparams(("parallel",)),
        name=name,
    )(x, mod, w, cos, sin)


def _attn_kernel(sink_ref, q_ref, km_ref, kp_ref, kn_ref, vm_ref, vp_ref, vn_ref, kvc_ref, o_ref, *, tq, seq, n_ctx):
    i = pl.program_id(1)
    nsub = tq // WINDOW
    span = 3 * WINDOW
    nkv = km_ref.shape[2] // LANES
    lane = lax.broadcasted_iota(jnp.int32, (1, LANES), 1)
    lo = lane < HEAD_DIM
    r = lax.broadcasted_iota(jnp.int32, (WINDOW, n_ctx + span), 0)
    c = lax.broadcasted_iota(jnp.int32, (WINDOW, n_ctx + span), 1) - n_ctx
    band = (c >= r) & (c <= r + 2 * WINDOW)
    valids = []
    for j in range(nsub):
        kpos = c + (i * tq + (j - 1) * WINDOW)
        valids.append((c < 0) | (band & (kpos >= 0) & (kpos < seq)))
    gidx = lax.broadcasted_iota(jnp.int32, (GQA_GROUP, 1, 1), 0)
    zero = jnp.zeros((), BF16)
    for kh in range(nkv):
        ks = slice(kh * LANES, (kh + 1) * LANES)
        kall = jnp.concatenate([kp_ref[0, :, ks], km_ref[0, :, ks], kn_ref[0, :, ks]], axis=0)
        vall = jnp.concatenate([vp_ref[0, :, ks], vm_ref[0, :, ks], vn_ref[0, :, ks]], axis=0)
        kc = kvc_ref[0, :, ks]
        vc = kvc_ref[0, :, nkv * LANES + kh * LANES:nkv * LANES + (kh + 1) * LANES]
        sink = jnp.full((GQA_GROUP, 1, 1), sink_ref[kh * GQA_GROUP], F32)
        for g in range(1, GQA_GROUP):
            sink = jnp.where(gidx == g, sink_ref[kh * GQA_GROUP + g], sink)
        for j in range(nsub):
            rows = slice(j * WINDOW, (j + 1) * WINDOW)
            kk = jnp.concatenate([kc, kall[j * WINDOW:j * WINDOW + span]], axis=0)
            vv = jnp.concatenate([vc, vall[j * WINDOW:j * WINDOW + span]], axis=0)
            parts = []
            for t in range(GQA_GROUP // 2):
                q2 = q_ref[0, rows, (2 * kh + t) * LANES:(2 * kh + t + 1) * LANES]
                parts += [jnp.where(lo, q2, zero), jnp.where(lo, zero, q2)]
            lhs = jnp.concatenate(parts, axis=0)
            s = lax.dot_general(lhs, kk, (((1,), (1,)), ((), ())), preferred_element_type=F32)
            s = s.reshape(GQA_GROUP, WINDOW, n_ctx + span)
            s = jnp.where(valids[j][None], s, MASK_VALUE)
            m = jnp.maximum(jnp.max(s, axis=-1, keepdims=True), sink)
            p = jnp.exp(s - m)
            denom = jnp.sum(p, axis=-1, keepdims=True) + jnp.exp(sink - m)
            o = jnp.dot(p.astype(BF16).reshape(GQA_GROUP * WINDOW, n_ctx + span), vv, preferred_element_type=F32)
            o = o.reshape(GQA_GROUP, WINDOW, LANES) / denom
            for t in range(GQA_GROUP // 2):
                o2 = jnp.where(lo, o[2 * t], o[2 * t + 1])
                o_ref[0, rows, (2 * kh + t) * LANES:(2 * kh + t + 1) * LANES] = o2.astype(o_ref.dtype)


def _attention(q, kd, vd, kvc, sink, *, tq):
    b, s, qd = q.shape
    kvd = kd.shape[2]
    n_ctx = kvc.shape[1]
    nblk = s // WINDOW
    per = tq // WINDOW
    kern = functools.partial(_attn_kernel, tq=tq, seq=s, n_ctx=n_ctx)
    main = lambda bi, i: (bi, i, 0)
    prev = lambda bi, i: (bi, jnp.maximum(i * per - 1, 0), 0)
    nxt = lambda bi, i: (bi, jnp.minimum((i + 1) * per, nblk - 1), 0)
    return pl.pallas_call(
        kern,
        grid=(b, s // tq),
        in_specs=[
            pl.BlockSpec(memory_space=pltpu.SMEM),
            pl.BlockSpec((1, tq, qd), main),
            pl.BlockSpec((1, tq, kvd), main),
            pl.BlockSpec((1, WINDOW, kvd), prev),
            pl.BlockSpec((1, WINDOW, kvd), nxt),
            pl.BlockSpec((1, tq, kvd), main),
            pl.BlockSpec((1, WINDOW, kvd), prev),
            pl.BlockSpec((1, WINDOW, kvd), nxt),
            pl.BlockSpec((1, n_ctx, 2 * kvd), lambda bi, i: (bi, 0, 0)),
        ],
        out_specs=pl.BlockSpec((1, tq, qd), main),
        out_shape=jax.ShapeDtypeStruct((b, s, qd), BF16),
        compiler_params=_cparams(("parallel", "parallel")),
        name="window_attention",
    )(sink, q, kd, kd, kd, vd, vd, vd, kvc)


def _oproj_kernel(a_ref, w_ref, x_ref, mod_ref, g_ref, b_ref, o_ref):
    mix = jnp.dot(a_ref[...], w_ref[...], preferred_element_type=F32)
    o_ref[...] = _deepnorm_ln(x_ref[...], mix, mod_ref[0][2:3, :], g_ref[...], b_ref[...])


def _oproj_ln(a, w, x, mod, g, b, *, rows_per_mod, tm):
    t, d = x.shape
    spm = rows_per_mod // tm
    return pl.pallas_call(
        _oproj_kernel,
        grid=(t // tm,),
        in_specs=[
            pl.BlockSpec((tm, a.shape[1]), lambda i: (i, 0)),
            _const_spec(w.shape),
            pl.BlockSpec((tm, d), lambda i: (i, 0)),
            pl.BlockSpec((1, MOD_ROWS, d), lambda i: (i // spm, 0, 0)),
            _const_spec((1, d)),
            _const_spec((1, d)),
        ],
        out_specs=pl.BlockSpec((tm, d), lambda i: (i, 0)),
        out_shape=jax.ShapeDtypeStruct((t, d), F32),
        compiler_params=_cparams(("parallel",)),
        name="oproj_ln",
    )(a, w, x, mod, g, b)


def _ffn_kernel(x_ref, mod_ref, w1_ref, w3_ref, w2_ref, g_ref, b_ref, o_ref, *, fc):
    m = mod_ref[0]
    x = x_ref[...]
    hb = (x * (1.0 + m[4:5, :]) + m[3:4, :]).astype(BF16)
    acc = jnp.zeros(x.shape, F32)
    for c0 in range(0, w1_ref.shape[1], fc):
        a = jnp.dot(hb, w1_ref[:, c0:c0 + fc], preferred_element_type=F32)
        bb = jnp.dot(hb, w3_ref[:, c0:c0 + fc], preferred_element_type=F32)
        gact = (_silu(a) * bb).astype(BF16)
        acc = acc + jnp.dot(gact, w2_ref[c0:c0 + fc, :], preferred_element_type=F32)
    o_ref[...] = _deepnorm_ln(x, acc, m[5:6, :], g_ref[...], b_ref[...])


def _ffn_ln(x, mod, w1, w3, w2, g, b, *, rows_per_mod, tm, fc):
    t, d = x.shape
    spm = rows_per_mod // tm
    return pl.pallas_call(
        functools.partial(_ffn_kernel, fc=fc),
        grid=(t // tm,),
        in_specs=[
            pl.BlockSpec((tm, d), lambda i: (i, 0)),
            pl.BlockSpec((1, MOD_ROWS, d), lambda i: (i // spm, 0, 0)),
            _const_spec(w1.shape),
            _const_spec(w3.shape),
            _const_spec(w2.shape),
            _const_spec((1, d)),
            _const_spec((1, d)),
        ],
        out_specs=pl.BlockSpec((tm, d), lambda i: (i, 0)),
        out_shape=jax.ShapeDtypeStruct((t, d), F32),
        compiler_params=_cparams(("parallel",)),
        name="dense_swiglu_ln",
    )(x, mod, w1, w3, w2, g, b)


def _convin_kernel(x_ref, mod_ref, w_ref, u_ref, bg_ref):
    m = mod_ref[0]
    d = x_ref.shape[1]
    hb = (x_ref[...] * (1.0 + m[1:2, :]) + m[0:1, :]).astype(BF16)
    y = jnp.dot(hb, w_ref[...], preferred_element_type=F32)
    bg_ref[...] = y[:, :d].astype(bg_ref.dtype)
    u_ref[...] = (y[:, d:2 * d] * y[:, 2 * d:]).astype(u_ref.dtype)


def _conv_in(x, mod, w, *, rows_per_mod, tm):
    t, d = x.shape
    spm = rows_per_mod // tm
    return pl.pallas_call(
        _convin_kernel,
        grid=(t // tm,),
        in_specs=[
            pl.BlockSpec((tm, d), lambda i: (i, 0)),
            pl.BlockSpec((1, MOD_ROWS, d), lambda i: (i // spm, 0, 0)),
            _const_spec(w.shape),
        ],
        out_specs=[pl.BlockSpec((tm, d), lambda i: (i, 0))] * 2,
        out_shape=[jax.ShapeDtypeStruct((t, d), BF16)] * 2,
        compiler_params=_cparams(("parallel",)),
        name="conv_in",
    )(x, mod, w)


HALO = 16
RT_G1, RT_G2, RT_E1, RT_E2 = 8, 9, 10, 11


def _convout_kernel(u_ref, up_ref, un_ref, bg_ref, cw_ref, w_ref, x_ref, mod_ref, g_ref, b_ref, wr_ref,
                    x_out, h_out, route_out):
    j = pl.program_id(1)
    nj = pl.num_programs(1)
    m = mod_ref[0]
    u = u_ref[0].astype(F32)
    tm = u.shape[0]
    row = lax.broadcasted_iota(jnp.int32, (tm, 1), 0)
    prev_row = jnp.where(j == 0, 0.0, up_ref[0, HALO - 1:HALO, :].astype(F32))
    next_row = jnp.where(j == nj - 1, 0.0, un_ref[0, 0:1, :].astype(F32))
    u_prev = jnp.where(row == 0, prev_row, pltpu.roll(u, 1, 0))
    u_next = jnp.where(row == tm - 1, next_row, pltpu.roll(u, tm - 1, 0))
    cw = cw_ref[...]
    y = cw[0:1, :] * u_prev + cw[1:2, :] * u + cw[2:3, :] * u_next
    z = (bg_ref[0].astype(F32) * y).astype(BF16)
    mix = jnp.dot(z, w_ref[...], preferred_element_type=F32)
    x_new = _deepnorm_ln(x_ref[0], mix, m[2:3, :], g_ref[...], b_ref[...])
    x_out[0] = x_new
    h = x_new * (1.0 + m[4:5, :]) + m[3:4, :]
    h_out[0] = h
    hb = h.astype(BF16)
    lg = jnp.dot(hb, wr_ref[...], preferred_element_type=F32)
    lane = lax.broadcasted_iota(jnp.int32, lg.shape, 1).astype(F32)
    neg = jnp.float32(-jnp.inf)
    lg = jnp.where(lane < N_EXPERTS, lg, neg)
    m1 = jnp.max(lg, axis=-1, keepdims=True)
    i1 = jnp.min(jnp.where(lg == m1, lane, float(LANES)), axis=-1, keepdims=True)
    lg2 = jnp.where(lane == i1, neg, lg)
    m2 = jnp.max(lg2, axis=-1, keepdims=True)
    i2 = jnp.min(jnp.where(lg2 == m2, lane, float(LANES)), axis=-1, keepdims=True)
    e2 = jnp.exp(m2 - m1)
    gate1 = 1.0 / (1.0 + e2)
    gate2 = e2 / (1.0 + e2)
    member = jnp.where((lane == i1) | (lane == i2), 1.0, 0.0)
    extra = jnp.where(lane == RT_G1, gate1, jnp.where(lane == RT_G2, gate2,
                      jnp.where(lane == RT_E1, i1, jnp.where(lane == RT_E2, i2, 0.0))))
    route_out[0] = member + extra


def _conv_out(u, bg, cw, w, x, mod, g, b, wr, *, tm):
    bsz, s, d = x.shape
    per = tm // HALO
    nh = s // HALO
    main = lambda bi, j: (bi, j, 0)
    return pl.pallas_call(
        _convout_kernel,
        grid=(bsz, s // tm),
        in_specs=[
            pl.BlockSpec((1, tm, d), main),
            pl.BlockSpec((1, HALO, d), lambda bi, j: (bi, jnp.maximum(j * per - 1, 0), 0)),
            pl.BlockSpec((1, HALO, d), lambda bi, j: (bi, jnp.minimum((j + 1) * per, nh - 1), 0)),
            pl.BlockSpec((1, tm, d), main),
            _const_spec(cw.shape),
            _const_spec(w.shape),
            pl.BlockSpec((1, tm, d), main),
            pl.BlockSpec((1, MOD_ROWS, d), lambda bi, j: (bi, 0, 0)),
            _const_spec((1, d)),
            _const_spec((1, d)),
            _const_spec(wr.shape),
        ],
        out_specs=[
            pl.BlockSpec((1, tm, d), main),
            pl.BlockSpec((1, tm, d), main),
            pl.BlockSpec((1, tm, LANES), main),
        ],
        out_shape=[
            jax.ShapeDtypeStruct((bsz, s, d), F32),
            jax.ShapeDtypeStruct((bsz, s, d), F32),
            jax.ShapeDtypeStruct((bsz, s, LANES), F32),
        ],
        compiler_params=_cparams(("parallel", "parallel")),
        name="conv_out_ln_router",
    )(u, u, u, bg, cw, w, x, mod, g, b, wr)


PLAN_ROWS = 8
MOE_TILE = 512


def _plan_kernel(rt_ref, plan_ref, cnt_ref, carry_ref):
    i = pl.program_id(0)

    @pl.when(i == 0)
    def _():
        carry_ref[...] = jnp.zeros_like(carry_ref)

    rt = rt_ref[...]
    tp = rt.shape[0]
    lane = lax.broadcasted_iota(jnp.int32, rt.shape, 1)
    lane_f = lane.astype(F32)
    member = jnp.where(lane < N_EXPERTS, rt, 0.0)
    r = lax.broadcasted_iota(jnp.int32, (tp, tp), 0)
    c = lax.broadcasted_iota(jnp.int32, (tp, tp), 1)
    tri = jnp.where(c <= r, 1.0, 0.0).astype(BF16)
    incl = jnp.dot(tri, member.astype(BF16), preferred_element_type=F32)
    before = incl - member + carry_ref[0:1, :]
    e1 = jnp.sum(jnp.where(lane == RT_E1, rt, 0.0), axis=-1, keepdims=True)
    e2 = jnp.sum(jnp.where(lane == RT_E2, rt, 0.0), axis=-1, keepdims=True)
    rank1 = jnp.sum(jnp.where(lane_f == e1, before, 0.0), axis=-1, keepdims=True)
    rank2 = jnp.sum(jnp.where(lane_f == e2, before, 0.0), axis=-1, keepdims=True)
    rec = jnp.where(lane == 0, rank1, jnp.where(lane == 1, rank2,
                    jnp.where(lane == 2, e1, jnp.where(lane == 3, e2, 0.0))))
    plan_ref[...] = rec.T[:PLAN_ROWS, :].astype(jnp.int32)
    carry_ref[0:1, :] = carry_ref[0:1, :] + incl[tp - 1:tp, :]
    cnt_ref[...] = carry_ref[...]


def _moe_plan(rt, *, tp):
    t = rt.shape[0]
    return pl.pallas_call(
        _plan_kernel,
        grid=(t // tp,),
        in_specs=[pl.BlockSpec((tp, LANES), lambda i: (i, 0))],
        out_specs=[
            pl.BlockSpec((PLAN_ROWS, tp), lambda i: (0, i)),
            pl.BlockSpec((8, LANES), lambda i: (0, 0)),
        ],
        out_shape=[
            jax.ShapeDtypeStruct((PLAN_ROWS, t), jnp.int32),
            jax.ShapeDtypeStruct((8, LANES), F32),
        ],
        scratch_shapes=[pltpu.VMEM((8, LANES), F32)],
        compiler_params=_cparams(("arbitrary",)),
        name="moe_plan",
    )(rt)


def _row_copy(src_ref, src_row, dst_ref, dst_row, sem):
    return pltpu.make_async_copy(src_ref.at[pl.ds(src_row, 1)], dst_ref.at[pl.ds(dst_row, 1)], sem)


DMA_UNROLL = 8


def _dispatch_kernel(d1_ref, d2_ref, padlo_ref, padhi_ref, na_ref, h_ref, xs_ref, zero_ref, sem, zsem, *, tile):
    i = pl.program_id(0)
    tm = h_ref.shape[0]
    n_tiles = xs_ref.shape[0] // tile

    @pl.when(i == 0)
    def _():
        zero_ref[...] = jnp.zeros_like(zero_ref)

        def zrow_start(r, carry):
            _row_copy(zero_ref, 0, xs_ref, r, zsem).start()
            return carry

        def zrow_wait(r, carry):
            _row_copy(zero_ref, 0, xs_ref, r, zsem).wait()
            return carry

        def ztile_start(k, carry):
            pltpu.make_async_copy(zero_ref, xs_ref.at[pl.ds(k * tile, tile)], zsem).start()
            return carry

        def ztile_wait(k, carry):
            pltpu.make_async_copy(zero_ref, xs_ref.at[pl.ds(k * tile, tile)], zsem).wait()
            return carry

        for e in range(N_EXPERTS):
            lax.fori_loop(padlo_ref[e], padhi_ref[e], zrow_start, 0)
            lax.fori_loop(padlo_ref[e], padhi_ref[e], zrow_wait, 0)
        lax.fori_loop(na_ref[0], n_tiles, ztile_start, 0)
        lax.fori_loop(na_ref[0], n_tiles, ztile_wait, 0)

    base = i * tm

    def start(r, carry):
        _row_copy(h_ref, r, xs_ref, d1_ref[base + r], sem).start()
        _row_copy(h_ref, r, xs_ref, d2_ref[base + r], sem).start()
        return carry

    def wait(r, carry):
        _row_copy(h_ref, r, xs_ref, d1_ref[base + r], sem).wait()
        _row_copy(h_ref, r, xs_ref, d2_ref[base + r], sem).wait()
        return carry

    lax.fori_loop(0, tm, start, 0, unroll=DMA_UNROLL)
    lax.fori_loop(0, tm, wait, 0, unroll=DMA_UNROLL)


def _moe_dispatch(h, dest1, dest2, pad_lo, pad_hi, n_active, *, n_rows, tm, tile):
    t, d = h.shape
    smem = pl.BlockSpec(memory_space=pltpu.SMEM)
    return pl.pallas_call(
        functools.partial(_dispatch_kernel, tile=tile),
        grid=(t // tm,),
        in_specs=[smem, smem, smem, smem, smem, pl.BlockSpec((tm, d), lambda i: (i, 0))],
        out_specs=pl.BlockSpec(memory_space=pl.ANY),
        out_shape=jax.ShapeDtypeStruct((n_rows, d), F32),
        scratch_shapes=[pltpu.VMEM((tile, d), F32), pltpu.SemaphoreType.DMA(()), pltpu.SemaphoreType.DMA(())],
        compiler_params=_cparams(("arbitrary",)),
        name="moe_dispatch",
    )(dest1, dest2, pad_lo, pad_hi, n_active, h)


def _expert_kernel(te_ref, na_ref, x_ref, w1_ref, w3_ref, w2_ref, o_ref, acc_ref):
    i = pl.program_id(0)
    c = pl.program_id(1)
    nc = pl.num_programs(1)

    @pl.when(i < na_ref[0])
    def _():
        xb = x_ref[...].astype(BF16)
        a = jnp.dot(xb, w1_ref[0], preferred_element_type=F32)
        bb = jnp.dot(xb, w3_ref[0], preferred_element_type=F32)
        gact = (_silu(a) * bb).astype(BF16)
        part = jnp.dot(gact, w2_ref[0], preferred_element_type=F32)

        @pl.when(c == 0)
        def _():
            acc_ref[...] = part

        @pl.when(c > 0)
        def _():
            acc_ref[...] += part

        @pl.when(c == nc - 1)
        def _():
            o_ref[...] = acc_ref[...]

    @pl.when((i >= na_ref[0]) & (c == nc - 1))
    def _():
        o_ref[...] = jnp.zeros_like(o_ref)


def _moe_experts(xs, tile_expert, n_active, w1, w3, w2, *, tm, fc):
    n_rows, d = xs.shape
    ff = w1.shape[2]
    nc = ff // fc

    def row_map(i, c, te, na):
        return (jnp.minimum(i, na[0] - 1), 0)

    def chunk(i, c, na):
        return jnp.where(i < na[0], c, nc - 1)

    grid_spec = pltpu.PrefetchScalarGridSpec(
        num_scalar_prefetch=2,
        grid=(n_rows // tm, nc),
        in_specs=[
            pl.BlockSpec((tm, d), row_map),
            pl.BlockSpec((1, d, fc), lambda i, c, te, na: (te[i], 0, chunk(i, c, na))),
            pl.BlockSpec((1, d, fc), lambda i, c, te, na: (te[i], 0, chunk(i, c, na))),
            pl.BlockSpec((1, fc, d), lambda i, c, te, na: (te[i], chunk(i, c, na), 0)),
        ],
        out_specs=pl.BlockSpec((tm, d), lambda i, c, te, na: (i, 0)),
        scratch_shapes=[pltpu.VMEM((tm, d), F32)],
    )
    return pl.pallas_call(
        _expert_kernel,
        grid_spec=grid_spec,
        out_shape=jax.ShapeDtypeStruct((n_rows, d), F32),
        compiler_params=_cparams(("arbitrary", "arbitrary")),
        name="moe_experts",
    )(tile_expert, n_active, xs, w1, w3, w2)


def _combine_kernel(d1_ref, d2_ref, ys_ref, rt_ref, x_ref, mod_ref, g_ref, b_ref, o_ref, y1_ref, y2_ref, sem):
    i = pl.program_id(0)
    tm = x_ref.shape[0]
    base = i * tm

    def start(r, carry):
        _row_copy(ys_ref, d1_ref[base + r], y1_ref, r, sem).start()
        _row_copy(ys_ref, d2_ref[base + r], y2_ref, r, sem).start()
        return carry

    def wait(r, carry):
        _row_copy(ys_ref, d1_ref[base + r], y1_ref, r, sem).wait()
        _row_copy(ys_ref, d2_ref[base + r], y2_ref, r, sem).wait()
        return carry

    lax.fori_loop(0, tm, start, 0, unroll=DMA_UNROLL)
    lax.fori_loop(0, tm, wait, 0, unroll=DMA_UNROLL)
    rt = rt_ref[...]
    y = rt[:, RT_G1:RT_G1 + 1] * y1_ref[...] + rt[:, RT_G2:RT_G2 + 1] * y2_ref[...]
    o_ref[...] = _deepnorm_ln(x_ref[...], y, mod_ref[0][5:6, :], g_ref[...], b_ref[...])


def _moe_combine_ln(ys, dest1, dest2, rt, x, mod, g, b, *, rows_per_mod, tm):
    t, d = x.shape
    spm = rows_per_mod // tm
    smem = pl.BlockSpec(memory_space=pltpu.SMEM)
    return pl.pallas_call(
        _combine_kernel,
        grid=(t // tm,),
        in_specs=[
            smem, smem,
            pl.BlockSpec(memory_space=pl.ANY),
            pl.BlockSpec((tm, LANES), lambda i: (i, 0)),
            pl.BlockSpec((tm, d), lambda i: (i, 0)),
            pl.BlockSpec((1, MOD_ROWS, d), lambda i: (i // spm, 0, 0)),
            _const_spec((1, d)),
            _const_spec((1, d)),
        ],
        out_specs=pl.BlockSpec((tm, d), lambda i: (i, 0)),
        out_shape=jax.ShapeDtypeStruct((t, d), F32),
        scratch_shapes=[pltpu.VMEM((tm, d), F32), pltpu.VMEM((tm, d), F32), pltpu.SemaphoreType.DMA(())],
        compiler_params=_cparams(("arbitrary",)),
        name="moe_combine_ln",
    )(dest1, dest2, ys, rt, x, mod, g, b)


def _moe_layout(plan, counts, *, tile):
    n_pairs = plan.shape[1] * 2
    n_tiles = n_pairs // tile + N_EXPERTS
    cnt = counts[0, :N_EXPERTS].astype(jnp.int32)
    padded = (cnt + tile - 1) // tile * tile
    ends = jnp.cumsum(padded)
    starts = ends - padded
    dest1 = starts[plan[2]] + plan[0]
    dest2 = starts[plan[3]] + plan[1]
    n_active = ends[-1:] // tile
    tiles = jnp.minimum(jnp.arange(n_tiles, dtype=jnp.int32), n_active[0] - 1) * tile
    tile_expert = jnp.minimum(jnp.searchsorted(ends, tiles, side="right"), N_EXPERTS - 1).astype(jnp.int32)
    return dest1, dest2, starts + cnt, ends, tile_expert, n_active.astype(jnp.int32), n_tiles * tile


def _rope_tables(rows_count):
    half = HEAD_DIM // 2
    rows = jnp.repeat(jnp.arange(rows_count, dtype=F32), GRID_W)
    cols = jnp.tile(jnp.arange(GRID_W, dtype=F32), rows_count)
    inv_freq = ROPE_THETA ** (-jnp.arange(0, half, 2, dtype=F32) / half)
    ar = rows[:, None] * inv_freq
    ac = cols[:, None] * inv_freq
    ang = jnp.concatenate([ar, ar, ac, ac], axis=-1)
    sign = jnp.where((jnp.arange(HEAD_DIM) % 32) < 16, -1.0, 1.0).astype(F32)
    cos = jnp.tile(jnp.cos(ang), (1, LANES // HEAD_DIM))
    sin = jnp.tile(jnp.sin(ang) * sign, (1, LANES // HEAD_DIM))
    return cos, sin


def _dup_heads(w, n_heads):
    d = w.shape[0]
    w = w.reshape(d, n_heads, 1, HEAD_DIM)
    return jnp.broadcast_to(w, (d, n_heads, 2, HEAD_DIM)).reshape(d, n_heads * 2 * HEAD_DIM)


def _mod_blocks(mod_rows, d):
    r = mod_rows.shape[0]
    m = mod_rows.reshape(r, 6, d)
    return jnp.concatenate([m, jnp.zeros((r, MOD_ROWS - 6, d), F32)], axis=1)


def kernel(x, c, ctx, c_ctx, w_mod, b_mod, ln_g, ln_b, attn_w_qkv, attn_w_o, attn_sink,
           conv_w_in, conv_w, conv_w_out, ffn_w1, ffn_w3, ffn_w2,
           moe_router, moe_w1, moe_w3, moe_w2):
    bsz, seq, d = x.shape
    n_ctx = ctx.shape[1]
    t = bsz * seq
    q_dim = attn_w_o.shape[1]
    kv_dim = (attn_w_qkv.shape[2] - q_dim) // 2
    n_kv = kv_dim // HEAD_DIM

    pad_rows = (-(bsz + 1)) % 8
    cvec = jnp.concatenate([c, c_ctx[None, :], jnp.zeros((pad_rows, d), F32)], axis=0)
    mod_all = _modulation(cvec, w_mod, b_mod)
    mod0 = _mod_blocks(mod_all[0, :bsz], d)
    modc0 = _mod_blocks(mod_all[0, bsz:bsz + 1], d)
    mod1 = _mod_blocks(mod_all[1, :bsz], d)

    cos, sin = _rope_tables(seq // GRID_W)

    wqkv = attn_w_qkv[0]
    wk = _dup_heads(wqkv[:, q_dim:q_dim + kv_dim], n_kv)
    wv = _dup_heads(wqkv[:, q_dim + kv_dim:], n_kv)
    w_all = jnp.concatenate([wqkv[:, :q_dim], wk, wv], axis=1).astype(BF16)
    w_kv = jnp.concatenate([wk, wv], axis=1).astype(BF16)
    x2d = x.reshape(t, d)
    q, kd, vd = _project(x2d, mod0, w_all, cos, sin, rows_per_mod=seq, tm=512,
                         rope_lanes=q_dim + 2 * kv_dim, q_lanes=q_dim,
                         splits=(q_dim, 2 * kv_dim, 2 * kv_dim), name="qkv_rope")
    (kvc,) = _project(ctx.reshape(bsz * n_ctx, d), modc0, w_kv, cos, sin, rows_per_mod=bsz * n_ctx, tm=512,
                      rope_lanes=0, q_lanes=0, splits=(4 * kv_dim,), name="ctx_kv")
    attn = _attention(q.reshape(bsz, seq, q_dim), kd.reshape(bsz, seq, 2 * kv_dim), vd.reshape(bsz, seq, 2 * kv_dim),
                      kvc.reshape(bsz, n_ctx, 4 * kv_dim), attn_sink[0], tq=256)
    x1 = _oproj_ln(attn.reshape(t, q_dim), attn_w_o[0].astype(BF16), x2d, mod0,
                   ln_g[0, 0][None], ln_b[0, 0][None], rows_per_mod=seq, tm=512)
    x2 = _ffn_ln(x1, mod0, ffn_w1[0].astype(BF16), ffn_w3[0].astype(BF16), ffn_w2[0].astype(BF16),
                 ln_g[0, 1][None], ln_b[0, 1][None], rows_per_mod=seq, tm=512, fc=256)

    u, bg = _conv_in(x2, mod1, conv_w_in[0].astype(BF16), rows_per_mod=seq, tm=512)
    cw = jnp.concatenate([conv_w[0], jnp.zeros((8 - CONV_WIDTH, d), F32)], axis=0)
    wr = jnp.concatenate([moe_router[0], jnp.zeros((d, LANES - N_EXPERTS), F32)], axis=1).astype(BF16)
    x3, h4, rt = _conv_out(u.reshape(bsz, seq, d), bg.reshape(bsz, seq, d), cw, conv_w_out[0].astype(BF16),
                           x2.reshape(bsz, seq, d), mod1, ln_g[1, 0][None], ln_b[1, 0][None], wr, tm=512)
    rt = rt.reshape(t, LANES)
    plan, counts = _moe_plan(rt, tp=512)
    dest1, dest2, pad_lo, pad_hi, tile_expert, n_active, n_rows = _moe_layout(plan, counts, tile=MOE_TILE)
    xs = _moe_dispatch(h4.reshape(t, d), dest1, dest2, pad_lo, pad_hi, n_active, n_rows=n_rows, tm=512,
                       tile=MOE_TILE)
    ys = _moe_experts(xs, tile_expert, n_active, moe_w1[0].astype(BF16), moe_w3[0].astype(BF16),
                      moe_w2[0].astype(BF16), tm=MOE_TILE, fc=512)
    out = _moe_combine_ln(ys, dest1, dest2, rt, x3.reshape(t, d), mod1, ln_g[1, 1][None], ln_b[1, 1][None],
                          rows_per_mod=seq, tm=512)
    return out.reshape(bsz, seq, d)
```

```python
import functools

import jax
import jax.numpy as jnp
from jax import lax
from jax.experimental import pallas as pl
from jax.experimental.pallas import tpu as pltpu

F32 = jnp.float32
BF16 = jnp.bfloat16

GRID_W = 64
HEAD_DIM = 64
GQA_GROUP = 4
WINDOW = 128
ROPE_THETA = 10000.0
CONV_WIDTH = 3
N_EXPERTS = 8
LN_EPS = 1e-5
MASK_VALUE = -1e30
DEPTH = 2
DEEPNORM_ALPHA = (2 * DEPTH) ** 0.25

LANES = 128
MOD_ROWS = 8
VMEM_LIMIT = 56 * 1024 * 1024


def _cparams(sem):
    return pltpu.CompilerParams(dimension_semantics=sem, vmem_limit_bytes=VMEM_LIMIT)


def _const_spec(shape):
    nd = len(shape)
    return pl.BlockSpec(shape, lambda *_: (0,) * nd, pipeline_mode=pl.Buffered(1))


def _silu(a):
    return a / (1.0 + jnp.exp(-a))


SUBLANES = 8


def _store_token_tiles(ref, val):
    for j in range(SUBLANES):
        ref[:, j, :] = val[:, j * LANES:(j + 1) * LANES]


def _load_token_tiles(ref):
    return jnp.concatenate([ref[:, j, :] for j in range(SUBLANES)], axis=-1)


def _deepnorm_ln(x, mix, gate, g, b):
    y = DEEPNORM_ALPHA * x + gate * mix
    mu = jnp.mean(y, axis=-1, keepdims=True)
    d = y - mu
    var = jnp.mean(d * d, axis=-1, keepdims=True)
    return d * lax.rsqrt(var + LN_EPS) * g + b


def _mod_kernel(c_ref, w_ref, b_ref, o_ref):
    cv = c_ref[...]
    s = _silu(cv).astype(BF16)
    o_ref[0] = jnp.dot(s, w_ref[0].astype(BF16), preferred_element_type=F32) + b_ref[0]


def _modulation(cvec, w_mod, b_mod):
    depth, d, n = w_mod.shape
    rows = cvec.shape[0]
    nt = 1536
    return pl.pallas_call(
        _mod_kernel,
        grid=(depth, n // nt),
        in_specs=[
            pl.BlockSpec((rows, d), lambda l, j: (0, 0)),
            pl.BlockSpec((1, d, nt), lambda l, j: (l, 0, j)),
            pl.BlockSpec((1, 1, nt), lambda l, j: (l, 0, j)),
        ],
        out_specs=pl.BlockSpec((1, rows, nt), lambda l, j: (l, 0, j)),
        out_shape=jax.ShapeDtypeStruct((depth, rows, n), F32),
        compiler_params=_cparams(("parallel", "parallel")),
        name="modulation",
    )(cvec, w_mod, b_mod.reshape(depth, 1, n))


def _proj_kernel(x_ref, mod_ref, w_ref, cos_ref, sin_ref, *o_refs, rope_lanes, q_lanes, splits):
    m = mod_ref[0]
    h = x_ref[...] * (1.0 + m[1:2, :]) + m[0:1, :]
    y = jnp.dot(h.astype(BF16), w_ref[...], preferred_element_type=F32)
    if rope_lanes:
        cos = cos_ref[...]
        sin = sin_ref[...]
        first_half = (lax.broadcasted_iota(jnp.int32, (1, LANES), 1) % 32) < 16
    off = 0
    for o_ref, width in zip(o_refs, splits):
        for c0 in range(0, width, LANES):
            yc = y[:, off + c0:off + c0 + LANES]
            if off + c0 < rope_lanes:
                rot = jnp.where(first_half, pltpu.roll(yc, LANES - 16, 1), pltpu.roll(yc, 16, 1))
                yc = yc * cos + rot * sin
                if off + c0 < q_lanes:
                    yc = yc * (HEAD_DIM ** -0.5)
            o_ref[:, c0:c0 + LANES] = yc.astype(o_ref.dtype)
        off += width


def _project(x, mod, w, cos, sin, *, rows_per_mod, tm, rope_lanes, q_lanes, splits, name):
    t, d = x.shape
    n = w.shape[1]
    steps_per_mod = rows_per_mod // tm
    steps_per_seq = cos.shape[0] // tm
    kern = functools.partial(_proj_kernel, rope_lanes=rope_lanes, q_lanes=q_lanes, splits=splits)
    return pl.pallas_call(
        kern,
        grid=(t // tm,),
        in_specs=[
            pl.BlockSpec((tm, d), lambda i: (i, 0)),
            pl.BlockSpec((1, MOD_ROWS, d), lambda i: (i // steps_per_mod, 0, 0)),
            _const_spec((d, n)),
            pl.BlockSpec((tm, LANES), lambda i: (i % steps_per_seq, 0)),
            pl.BlockSpec((tm, LANES), lambda i: (i % steps_per_seq, 0)),
        ],
        out_specs=[pl.BlockSpec((tm, s), lambda i: (i, 0)) for s in splits],
        out_shape=[jax.ShapeDtypeStruct((t, s), BF16) for s in splits],
        compiler_params=_cparams(("parallel",)),
        name=name,
    )(x, mod, w, cos, sin)


def _attn_kernel(sink_ref, q_ref, km_ref, kp_ref, kn_ref, vm_ref, vp_ref, vn_ref, kvc_ref, o_ref, *, tq, seq, n_ctx):
    i = pl.program_id(1)
    nsub = tq // WINDOW
    span = 3 * WINDOW
    nkv = km_ref.shape[2] // LANES
    lane = lax.broadcasted_iota(jnp.int32, (1, LANES), 1)
    lo = lane < HEAD_DIM
    r = lax.broadcasted_iota(jnp.int32, (WINDOW, n_ctx + span), 0)
    c = lax.broadcasted_iota(jnp.int32, (WINDOW, n_ctx + span), 1) - n_ctx
    band = (c >= r) & (c <= r + 2 * WINDOW)
    valids = []
    for j in range(nsub):
        kpos = c + (i * tq + (j - 1) * WINDOW)
        valids.append((c < 0) | (band & (kpos >= 0) & (kpos < seq)))
    gidx = lax.broadcasted_iota(jnp.int32, (GQA_GROUP, 1, 1), 0)
    zero = jnp.zeros((), BF16)
    for kh in range(nkv):
        ks = slice(kh * LANES, (kh + 1) * LANES)
        kall = jnp.concatenate([kp_ref[0, :, ks], km_ref[0, :, ks], kn_ref[0, :, ks]], axis=0)
        vall = jnp.concatenate([vp_ref[0, :, ks], vm_ref[0, :, ks], vn_ref[0, :, ks]], axis=0)
        kc = kvc_ref[0, :, ks]
        vc = kvc_ref[0, :, nkv * LANES + kh * LANES:nkv * LANES + (kh + 1) * LANES]
        sink = jnp.full((GQA_GROUP, 1, 1), sink_ref[kh * GQA_GROUP], F32)
        for g in range(1, GQA_GROUP):
            sink = jnp.where(gidx == g, sink_ref[kh * GQA_GROUP + g], sink)
        for j in range(nsub):
            rows = slice(j * WINDOW, (j + 1) * WINDOW)
            kk = jnp.concatenate([kc, kall[j * WINDOW:j * WINDOW + span]], axis=0)
            vv = jnp.concatenate([vc, vall[j * WINDOW:j * WINDOW + span]], axis=0)
            parts = []
            for t in range(GQA_GROUP // 2):
                q2 = q_ref[0, rows, (2 * kh + t) * LANES:(2 * kh + t + 1) * LANES]
                parts += [jnp.where(lo, q2, zero), jnp.where(lo, zero, q2)]
            lhs = jnp.concatenate(parts, axis=0)
            s = lax.dot_general(lhs, kk, (((1,), (1,)), ((), ())), preferred_element_type=F32)
            s = s.reshape(GQA_GROUP, WINDOW, n_ctx + span)
            s = jnp.where(valids[j][None], s, MASK_VALUE)
            m = jnp.maximum(jnp.max(s, axis=-1, keepdims=True), sink)
            p = jnp.exp(s - m)
            denom = jnp.sum(p, axis=-1, keepdims=True) + jnp.exp(sink - m)
            o = jnp.dot(p.astype(BF16).reshape(GQA_GROUP * WINDOW, n_ctx + span), vv, preferred_element_type=F32)
            o = o.reshape(GQA_GROUP, WINDOW, LANES) / denom
            for t in range(GQA_GROUP // 2):
                o2 = jnp.where(lo, o[2 * t], o[2 * t + 1])
                o_ref[0, rows, (2 * kh + t) * LANES:(2 * kh + t + 1) * LANES] = o2.astype(o_ref.dtype)


def _attention(q, kd, vd, kvc, sink, *, tq):
    b, s, qd = q.shape
    kvd = kd.shape[2]
    n_ctx = kvc.shape[1]
    nblk = s // WINDOW
    per = tq // WINDOW
    kern = functools.partial(_attn_kernel, tq=tq, seq=s, n_ctx=n_ctx)
    main = lambda bi, i: (bi, i, 0)
    prev = lambda bi, i: (bi, jnp.maximum(i * per - 1, 0), 0)
    nxt = lambda bi, i: (bi, jnp.minimum((i + 1) * per, nblk - 1), 0)
    return pl.pallas_call(
        kern,
        grid=(b, s // tq),
        in_specs=[
            pl.BlockSpec(memory_space=pltpu.SMEM),
            pl.BlockSpec((1, tq, qd), main),
            pl.BlockSpec((1, tq, kvd), main),
            pl.BlockSpec((1, WINDOW, kvd), prev),
            pl.BlockSpec((1, WINDOW, kvd), nxt),
            pl.BlockSpec((1, tq, kvd), main),
            pl.BlockSpec((1, WINDOW, kvd), prev),
            pl.BlockSpec((1, WINDOW, kvd), nxt),
            pl.BlockSpec((1, n_ctx, 2 * kvd), lambda bi, i: (bi, 0, 0)),
        ],
        out_specs=pl.BlockSpec((1, tq, qd), main),
        out_shape=jax.ShapeDtypeStruct((b, s, qd), BF16),
        compiler_params=_cparams(("parallel", "parallel")),
        name="window_attention",
    )(sink, q, kd, kd, kd, vd, vd, vd, kvc)


def _oproj_kernel(a_ref, w_ref, x_ref, mod_ref, g_ref, b_ref, o_ref):
    mix = jnp.dot(a_ref[...], w_ref[...], preferred_element_type=F32)
    o_ref[...] = _deepnorm_ln(x_ref[...], mix, mod_ref[0][2:3, :], g_ref[...], b_ref[...])


def _oproj_ln(a, w, x, mod, g, b, *, rows_per_mod, tm):
    t, d = x.shape
    spm = rows_per_mod // tm
    return pl.pallas_call(
        _oproj_kernel,
        grid=(t // tm,),
        in_specs=[
            pl.BlockSpec((tm, a.shape[1]), lambda i: (i, 0)),
            _const_spec(w.shape),
            pl.BlockSpec((tm, d), lambda i: (i, 0)),
            pl.BlockSpec((1, MOD_ROWS, d), lambda i: (i // spm, 0, 0)),
            _const_spec((1, d)),
            _const_spec((1, d)),
        ],
        out_specs=pl.BlockSpec((tm, d), lambda i: (i, 0)),
        out_shape=jax.ShapeDtypeStruct((t, d), F32),
        compiler_params=_cparams(("parallel",)),
        name="oproj_ln",
    )(a, w, x, mod, g, b)


def _ffn_kernel(x_ref, mod_ref, w1_ref, w3_ref, w2_ref, g_ref, b_ref, o_ref, *, fc):
    m = mod_ref[0]
    x = x_ref[...]
    hb = (x * (1.0 + m[4:5, :]) + m[3:4, :]).astype(BF16)
    acc = jnp.zeros(x.shape, F32)
    for c0 in range(0, w1_ref.shape[1], fc):
        a = jnp.dot(hb, w1_ref[:, c0:c0 + fc], preferred_element_type=F32)
        bb = jnp.dot(hb, w3_ref[:, c0:c0 + fc], preferred_element_type=F32)
        gact = (_silu(a) * bb).astype(BF16)
        acc = acc + jnp.dot(gact, w2_ref[c0:c0 + fc, :], preferred_element_type=F32)
    o_ref[...] = _deepnorm_ln(x, acc, m[5:6, :], g_ref[...], b_ref[...])


def _ffn_ln(x, mod, w1, w3, w2, g, b, *, rows_per_mod, tm, fc):
    t, d = x.shape
    spm = rows_per_mod // tm
    return pl.pallas_call(
        functools.partial(_ffn_kernel, fc=fc),
        grid=(t // tm,),
        in_specs=[
            pl.BlockSpec((tm, d), lambda i: (i, 0)),
            pl.BlockSpec((1, MOD_ROWS, d), lambda i: (i // spm, 0, 0)),
            _const_spec(w1.shape),
            _const_spec(w3.shape),
            _const_spec(w2.shape),
            _const_spec((1, d)),
            _const_spec((1, d)),
        ],
        out_specs=pl.BlockSpec((tm, d), lambda i: (i, 0)),
        out_shape=jax.ShapeDtypeStruct((t, d), F32),
        compiler_params=_cparams(("parallel",)),
        name="dense_swiglu_ln",
    )(x, mod, w1, w3, w2, g, b)


def _convin_kernel(x_ref, mod_ref, w_ref, u_ref, bg_ref):
    m = mod_ref[0]
    d = x_ref.shape[1]
    hb = (x_ref[...] * (1.0 + m[1:2, :]) + m[0:1, :]).astype(BF16)
    y = jnp.dot(hb, w_ref[...], preferred_element_type=F32)
    bg_ref[...] = y[:, :d].astype(bg_ref.dtype)
    u_ref[...] = (y[:, d:2 * d] * y[:, 2 * d:]).astype(u_ref.dtype)


def _conv_in(x, mod, w, *, rows_per_mod, tm):
    t, d = x.shape
    spm = rows_per_mod // tm
    return pl.pallas_call(
        _convin_kernel,
        grid=(t // tm,),
        in_specs=[
            pl.BlockSpec((tm, d), lambda i: (i, 0)),
            pl.BlockSpec((1, MOD_ROWS, d), lambda i: (i // spm, 0, 0)),
            _const_spec(w.shape),
        ],
        out_specs=[pl.BlockSpec((tm, d), lambda i: (i, 0))] * 2,
        out_shape=[jax.ShapeDtypeStruct((t, d), BF16)] * 2,
        compiler_params=_cparams(("parallel",)),
        name="conv_in",
    )(x, mod, w)


HALO = 16
RT_G1, RT_G2, RT_E1, RT_E2 = 8, 9, 10, 11


def _convout_kernel(u_ref, up_ref, un_ref, bg_ref, cw_ref, w_ref, x_ref, mod_ref, g_ref, b_ref, wr_ref,
                    x_out, h_out, route_out):
    j = pl.program_id(1)
    nj = pl.num_programs(1)
    m = mod_ref[0]
    u = u_ref[0].astype(F32)
    tm = u.shape[0]
    row = lax.broadcasted_iota(jnp.int32, (tm, 1), 0)
    prev_row = jnp.where(j == 0, 0.0, up_ref[0, HALO - 1:HALO, :].astype(F32))
    next_row = jnp.where(j == nj - 1, 0.0, un_ref[0, 0:1, :].astype(F32))
    u_prev = jnp.where(row == 0, prev_row, pltpu.roll(u, 1, 0))
    u_next = jnp.where(row == tm - 1, next_row, pltpu.roll(u, tm - 1, 0))
    cw = cw_ref[...]
    y = cw[0:1, :] * u_prev + cw[1:2, :] * u + cw[2:3, :] * u_next
    z = (bg_ref[0].astype(F32) * y).astype(BF16)
    mix = jnp.dot(z, w_ref[...], preferred_element_type=F32)
    x_new = _deepnorm_ln(x_ref[0], mix, m[2:3, :], g_ref[...], b_ref[...])
    x_out[0] = x_new
    h = x_new * (1.0 + m[4:5, :]) + m[3:4, :]
    for c in range(SUBLANES):
        h_out[0, c] = h[:, c * LANES:(c + 1) * LANES]
    hb = h.astype(BF16)
    lg = jnp.dot(hb, wr_ref[...], preferred_element_type=F32)
    lane = lax.broadcasted_iota(jnp.int32, lg.shape, 1).astype(F32)
    neg = jnp.float32(-jnp.inf)
    lg = jnp.where(lane < N_EXPERTS, lg, neg)
    m1 = jnp.max(lg, axis=-1, keepdims=True)
    i1 = jnp.min(jnp.where(lg == m1, lane, float(LANES)), axis=-1, keepdims=True)
    lg2 = jnp.where(lane == i1, neg, lg)
    m2 = jnp.max(lg2, axis=-1, keepdims=True)
    i2 = jnp.min(jnp.where(lg2 == m2, lane, float(LANES)), axis=-1, keepdims=True)
    e2 = jnp.exp(m2 - m1)
    gate1 = 1.0 / (1.0 + e2)
    gate2 = e2 / (1.0 + e2)
    member = jnp.where((lane == i1) | (lane == i2), 1.0, 0.0)
    extra = jnp.where(lane == RT_G1, gate1, jnp.where(lane == RT_G2, gate2,
                      jnp.where(lane == RT_E1, i1, jnp.where(lane == RT_E2, i2, 0.0))))
    route_out[0] = member + extra


def _conv_out(u, bg, cw, w, x, mod, g, b, wr, *, tm):
    bsz, s, d = x.shape
    per = tm // HALO
    nh = s // HALO
    main = lambda bi, j: (bi, j, 0)
    return pl.pallas_call(
        _convout_kernel,
        grid=(bsz, s // tm),
        in_specs=[
            pl.BlockSpec((1, tm, d), main),
            pl.BlockSpec((1, HALO, d), lambda bi, j: (bi, jnp.maximum(j * per - 1, 0), 0)),
            pl.BlockSpec((1, HALO, d), lambda bi, j: (bi, jnp.minimum((j + 1) * per, nh - 1), 0)),
            pl.BlockSpec((1, tm, d), main),
            _const_spec(cw.shape),
            _const_spec(w.shape),
            pl.BlockSpec((1, tm, d), main),
            pl.BlockSpec((1, MOD_ROWS, d), lambda bi, j: (bi, 0, 0)),
            _const_spec((1, d)),
            _const_spec((1, d)),
            _const_spec(wr.shape),
        ],
        out_specs=[
            pl.BlockSpec((1, tm, d), main),
            pl.BlockSpec((1, SUBLANES, tm, d // SUBLANES), lambda bi, j: (bi, 0, j, 0)),
            pl.BlockSpec((1, tm, LANES), main),
        ],
        out_shape=[
            jax.ShapeDtypeStruct((bsz, s, d), F32),
            jax.ShapeDtypeStruct((bsz, SUBLANES, s, d // SUBLANES), F32),
            jax.ShapeDtypeStruct((bsz, s, LANES), F32),
        ],
        compiler_params=_cparams(("parallel", "parallel")),
        name="conv_out_ln_router",
    )(u, u, u, bg, cw, w, x, mod, g, b, wr)


PLAN_ROWS = 8
MOE_TILE = 512


def _plan_kernel(rt_ref, plan_ref, cnt_ref, carry_ref):
    i = pl.program_id(0)

    @pl.when(i == 0)
    def _():
        carry_ref[...] = jnp.zeros_like(carry_ref)

    rt = rt_ref[...]
    tp = rt.shape[0]
    lane = lax.broadcasted_iota(jnp.int32, rt.shape, 1)
    lane_f = lane.astype(F32)
    member = jnp.where(lane < N_EXPERTS, rt, 0.0)
    r = lax.broadcasted_iota(jnp.int32, (tp, tp), 0)
    c = lax.broadcasted_iota(jnp.int32, (tp, tp), 1)
    tri = jnp.where(c <= r, 1.0, 0.0).astype(BF16)
    incl = jnp.dot(tri, member.astype(BF16), preferred_element_type=F32)
    before = incl - member + carry_ref[0:1, :]
    e1 = jnp.sum(jnp.where(lane == RT_E1, rt, 0.0), axis=-1, keepdims=True)
    e2 = jnp.sum(jnp.where(lane == RT_E2, rt, 0.0), axis=-1, keepdims=True)
    rank1 = jnp.sum(jnp.where(lane_f == e1, before, 0.0), axis=-1, keepdims=True)
    rank2 = jnp.sum(jnp.where(lane_f == e2, before, 0.0), axis=-1, keepdims=True)
    rec = jnp.where(lane == 0, rank1, jnp.where(lane == 1, rank2,
                    jnp.where(lane == 2, e1, jnp.where(lane == 3, e2, 0.0))))
    plan_ref[...] = rec.T[:PLAN_ROWS, :].astype(jnp.int32)
    carry_ref[0:1, :] = carry_ref[0:1, :] + incl[tp - 1:tp, :]
    cnt_ref[...] = carry_ref[...]


def _moe_plan(rt, *, tp):
    t = rt.shape[0]
    return pl.pallas_call(
        _plan_kernel,
        grid=(t // tp,),
        in_specs=[pl.BlockSpec((tp, LANES), lambda i: (i, 0))],
        out_specs=[
            pl.BlockSpec((PLAN_ROWS, tp), lambda i: (0, i)),
            pl.BlockSpec((8, LANES), lambda i: (0, 0)),
        ],
        out_shape=[
            jax.ShapeDtypeStruct((PLAN_ROWS, t), jnp.int32),
            jax.ShapeDtypeStruct((8, LANES), F32),
        ],
        scratch_shapes=[pltpu.VMEM((8, LANES), F32)],
        compiler_params=_cparams(("arbitrary",)),
        name="moe_plan",
    )(rt)


def _row_copy(src_ref, src_row, dst_ref, dst_row, sem):
    return pltpu.make_async_copy(src_ref.at[src_row], dst_ref.at[dst_row], sem)


def _chunked_row(ref, row):
    return ref.at[:, row, :]


DMA_UNROLL = 8


def _dispatch_kernel(d1_ref, d2_ref, padlo_ref, padhi_ref, na_ref, h_ref, xs_ref, zero_ref, sem, sem2, zsem, *, tile):
    i = pl.program_id(0)
    tm = h_ref.shape[2]
    hsrc = h_ref.at[0]
    n_tiles = xs_ref.shape[0] // tile

    @pl.when(i == 0)
    def _():
        zero_ref[...] = jnp.zeros_like(zero_ref)

        def zrow_start(r, carry):
            _row_copy(zero_ref, 0, xs_ref, r, zsem).start()
            return carry

        def zrow_wait(r, carry):
            _row_copy(zero_ref, 0, xs_ref, r, zsem).wait()
            return carry

        def ztile_start(k, carry):
            pltpu.make_async_copy(zero_ref, xs_ref.at[pl.ds(k * tile, tile)], zsem).start()
            return carry

        def ztile_wait(k, carry):
            pltpu.make_async_copy(zero_ref, xs_ref.at[pl.ds(k * tile, tile)], zsem).wait()
            return carry

        for e in range(N_EXPERTS):
            lax.fori_loop(padlo_ref[e], padhi_ref[e], zrow_start, 0)
            lax.fori_loop(padlo_ref[e], padhi_ref[e], zrow_wait, 0)
        lax.fori_loop(na_ref[0], n_tiles, ztile_start, 0)
        lax.fori_loop(na_ref[0], n_tiles, ztile_wait, 0)

    base = i * tm

    def start(r, carry):
        pltpu.make_async_copy(_chunked_row(hsrc, r), xs_ref.at[d1_ref[base + r]], sem).start(priority=0)
        pltpu.make_async_copy(_chunked_row(hsrc, r), xs_ref.at[d2_ref[base + r]], sem2).start(priority=1)
        return carry

    def wait(r, carry):
        pltpu.make_async_copy(_chunked_row(hsrc, r), xs_ref.at[d1_ref[base + r]], sem).wait()
        pltpu.make_async_copy(_chunked_row(hsrc, r), xs_ref.at[d2_ref[base + r]], sem2).wait()
        return carry

    lax.fori_loop(0, tm, start, 0, unroll=DMA_UNROLL)
    lax.fori_loop(0, tm, wait, 0, unroll=DMA_UNROLL)


def _moe_dispatch(h, dest1, dest2, pad_lo, pad_hi, n_active, *, n_rows, tm, tile):
    bsz, nch, seq, lanes = h.shape
    per = seq // tm
    smem = pl.BlockSpec(memory_space=pltpu.SMEM)
    dma = pltpu.SemaphoreType.DMA(())
    return pl.pallas_call(
        functools.partial(_dispatch_kernel, tile=tile),
        grid=(bsz * per,),
        in_specs=[smem, smem, smem, smem, smem,
                  pl.BlockSpec((1, nch, tm, lanes), lambda i: (i // per, 0, i % per, 0))],
        out_specs=pl.BlockSpec(memory_space=pl.ANY),
        out_shape=jax.ShapeDtypeStruct((n_rows, nch, lanes), F32),
        scratch_shapes=[pltpu.VMEM((tile, nch, lanes), F32), dma, dma, dma],
        compiler_params=_cparams(("arbitrary",)),
        name="moe_dispatch",
    )(dest1, dest2, pad_lo, pad_hi, n_active, h)


def _expert_kernel(te_ref, na_ref, x_ref, w1_ref, w3_ref, w2_ref, o_ref, *, fc):
    i = pl.program_id(0)

    @pl.when(i < na_ref[0])
    def _():
        xb = _load_token_tiles(x_ref).astype(BF16)
        acc = jnp.zeros(xb.shape, F32)
        for c0 in range(0, w1_ref.shape[2], fc):
            a = jnp.dot(xb, w1_ref[0, :, c0:c0 + fc], preferred_element_type=F32)
            bb = jnp.dot(xb, w3_ref[0, :, c0:c0 + fc], preferred_element_type=F32)
            gact = (_silu(a) * bb).astype(BF16)
            acc = acc + jnp.dot(gact, w2_ref[0, c0:c0 + fc, :], preferred_element_type=F32)
        _store_token_tiles(o_ref, acc)

    @pl.when(i >= na_ref[0])
    def _():
        o_ref[...] = jnp.zeros_like(o_ref)


def _moe_experts(xs, tile_expert, n_active, w1, w3, w2, *, tm, fc):
    n_rows = xs.shape[0]
    row = xs.shape[1:]
    _, d, ff = w1.shape

    def wspec(shape):
        return pl.BlockSpec(shape, lambda i, te, na: (te[i], 0, 0), pipeline_mode=pl.Buffered(1))

    grid_spec = pltpu.PrefetchScalarGridSpec(
        num_scalar_prefetch=2,
        grid=(n_rows // tm,),
        in_specs=[
            pl.BlockSpec((tm,) + row, lambda i, te, na: (jnp.minimum(i, na[0] - 1), 0, 0)),
            wspec((1, d, ff)),
            wspec((1, d, ff)),
            wspec((1, ff, d)),
        ],
        out_specs=pl.BlockSpec((tm,) + row, lambda i, te, na: (i, 0, 0)),
    )
    return pl.pallas_call(
        functools.partial(_expert_kernel, fc=fc),
        grid_spec=grid_spec,
        out_shape=jax.ShapeDtypeStruct((n_rows,) + row, F32),
        compiler_params=_cparams(("arbitrary",)),
        name="moe_experts",
    )(tile_expert, n_active, xs, w1, w3, w2)


def _combine_kernel(d1_ref, d2_ref, ys_ref, rt_ref, x_ref, mod_ref, g_ref, b_ref, o_ref, y1_ref, y2_ref, sem, sem2):
    i = pl.program_id(0)
    tm = x_ref.shape[0]
    base = i * tm

    def start(r, carry):
        pltpu.make_async_copy(ys_ref.at[d1_ref[base + r]], _chunked_row(y1_ref, r), sem).start(priority=0)
        pltpu.make_async_copy(ys_ref.at[d2_ref[base + r]], _chunked_row(y2_ref, r), sem2).start(priority=1)
        return carry

    def wait(r, carry):
        pltpu.make_async_copy(ys_ref.at[d1_ref[base + r]], _chunked_row(y1_ref, r), sem).wait()
        pltpu.make_async_copy(ys_ref.at[d2_ref[base + r]], _chunked_row(y2_ref, r), sem2).wait()
        return carry

    lax.fori_loop(0, tm, start, 0, unroll=DMA_UNROLL)
    lax.fori_loop(0, tm, wait, 0, unroll=DMA_UNROLL)
    rt = rt_ref[...]
    y1 = jnp.concatenate([y1_ref[c] for c in range(SUBLANES)], axis=-1)
    y2 = jnp.concatenate([y2_ref[c] for c in range(SUBLANES)], axis=-1)
    y = rt[:, RT_G1:RT_G1 + 1] * y1 + rt[:, RT_G2:RT_G2 + 1] * y2
    o_ref[...] = _deepnorm_ln(x_ref[...], y, mod_ref[0][5:6, :], g_ref[...], b_ref[...])


def _moe_combine_ln(ys, dest1, dest2, rt, x, mod, g, b, *, rows_per_mod, tm):
    t, d = x.shape
    row = ys.shape[1:]
    spm = rows_per_mod // tm
    smem = pl.BlockSpec(memory_space=pltpu.SMEM)
    dma = pltpu.SemaphoreType.DMA(())
    return pl.pallas_call(
        _combine_kernel,
        grid=(t // tm,),
        in_specs=[
            smem, smem,
            pl.BlockSpec(memory_space=pl.ANY),
            pl.BlockSpec((tm, LANES), lambda i: (i, 0)),
            pl.BlockSpec((tm, d), lambda i: (i, 0)),
            pl.BlockSpec((1, MOD_ROWS, d), lambda i: (i // spm, 0, 0)),
            _const_spec((1, d)),
            _const_spec((1, d)),
        ],
        out_specs=pl.BlockSpec((tm, d), lambda i: (i, 0)),
        out_shape=jax.ShapeDtypeStruct((t, d), F32),
        scratch_shapes=[pltpu.VMEM((row[0], tm, row[1]), F32), pltpu.VMEM((row[0], tm, row[1]), F32), dma, dma],
        compiler_params=_cparams(("arbitrary",)),
        name="moe_combine_ln",
    )(dest1, dest2, ys, rt, x, mod, g, b)


def _moe_layout(plan, counts, *, tile):
    n_pairs = plan.shape[1] * 2
    n_tiles = n_pairs // tile + N_EXPERTS
    cnt = counts[0, :N_EXPERTS].astype(jnp.int32)
    padded = (cnt + tile - 1) // tile * tile
    ends = jnp.cumsum(padded)
    starts = ends - padded
    dest1 = starts[plan[2]] + plan[0]
    dest2 = starts[plan[3]] + plan[1]
    n_active = ends[-1:] // tile
    tiles = jnp.minimum(jnp.arange(n_tiles, dtype=jnp.int32), n_active[0] - 1) * tile
    tile_expert = jnp.minimum(jnp.searchsorted(ends, tiles, side="right"), N_EXPERTS - 1).astype(jnp.int32)
    return dest1, dest2, starts + cnt, ends, tile_expert, n_active.astype(jnp.int32), n_tiles * tile


def _rope_tables(rows_count):
    half = HEAD_DIM // 2
    rows = jnp.repeat(jnp.arange(rows_count, dtype=F32), GRID_W)
    cols = jnp.tile(jnp.arange(GRID_W, dtype=F32), rows_count)
    inv_freq = ROPE_THETA ** (-jnp.arange(0, half, 2, dtype=F32) / half)
    ar = rows[:, None] * inv_freq
    ac = cols[:, None] * inv_freq
    ang = jnp.concatenate([ar, ar, ac, ac], axis=-1)
    sign = jnp.where((jnp.arange(HEAD_DIM) % 32) < 16, -1.0, 1.0).astype(F32)
    cos = jnp.tile(jnp.cos(ang), (1, LANES // HEAD_DIM))
    sin = jnp.tile(jnp.sin(ang) * sign, (1, LANES // HEAD_DIM))
    return cos, sin


def _dup_heads(w, n_heads):
    d = w.shape[0]
    w = w.reshape(d, n_heads, 1, HEAD_DIM)
    return jnp.broadcast_to(w, (d, n_heads, 2, HEAD_DIM)).reshape(d, n_heads * 2 * HEAD_DIM)


def _mod_blocks(mod_rows, d):
    r = mod_rows.shape[0]
    m = mod_rows.reshape(r, 6, d)
    return jnp.concatenate([m, jnp.zeros((r, MOD_ROWS - 6, d), F32)], axis=1)


def kernel(x, c, ctx, c_ctx, w_mod, b_mod, ln_g, ln_b, attn_w_qkv, attn_w_o, attn_sink,
           conv_w_in, conv_w, conv_w_out, ffn_w1, ffn_w3, ffn_w2,
           moe_router, moe_w1, moe_w3, moe_w2):
    bsz, seq, d = x.shape
    n_ctx = ctx.shape[1]
    t = bsz * seq
    q_dim = attn_w_o.shape[1]
    kv_dim = (attn_w_qkv.shape[2] - q_dim) // 2
    n_kv = kv_dim // HEAD_DIM

    pad_rows = (-(bsz + 1)) % 8
    cvec = jnp.concatenate([c, c_ctx[None, :], jnp.zeros((pad_rows, d), F32)], axis=0)
    mod_all = _modulation(cvec, w_mod, b_mod)
    mod0 = _mod_blocks(mod_all[0, :bsz], d)
    modc0 = _mod_blocks(mod_all[0, bsz:bsz + 1], d)
    mod1 = _mod_blocks(mod_all[1, :bsz], d)

    cos, sin = _rope_tables(seq // GRID_W)

    wqkv = attn_w_qkv[0]
    wk = _dup_heads(wqkv[:, q_dim:q_dim + kv_dim], n_kv)
    wv = _dup_heads(wqkv[:, q_dim + kv_dim:], n_kv)
    w_all = jnp.concatenate([wqkv[:, :q_dim], wk, wv], axis=1).astype(BF16)
    w_kv = jnp.concatenate([wk, wv], axis=1).astype(BF16)
    x2d = x.reshape(t, d)
    q, kd, vd = _project(x2d, mod0, w_all, cos, sin, rows_per_mod=seq, tm=512,
                         rope_lanes=q_dim + 2 * kv_dim, q_lanes=q_dim,
                         splits=(q_dim, 2 * kv_dim, 2 * kv_dim), name="qkv_rope")
    (kvc,) = _project(ctx.reshape(bsz * n_ctx, d), modc0, w_kv, cos, sin, rows_per_mod=bsz * n_ctx, tm=512,
                      rope_lanes=0, q_lanes=0, splits=(4 * kv_dim,), name="ctx_kv")
    attn = _attention(q.reshape(bsz, seq, q_dim), kd.reshape(bsz, seq, 2 * kv_dim), vd.reshape(bsz, seq, 2 * kv_dim),
                      kvc.reshape(bsz, n_ctx, 4 * kv_dim), attn_sink[0], tq=256)
    x1 = _oproj_ln(attn.reshape(t, q_dim), attn_w_o[0].astype(BF16), x2d, mod0,
                   ln_g[0, 0][None], ln_b[0, 0][None], rows_per_mod=seq, tm=512)
    x2 = _ffn_ln(x1, mod0, ffn_w1[0].astype(BF16), ffn_w3[0].astype(BF16), ffn_w2[0].astype(BF16),
                 ln_g[0, 1][None], ln_b[0, 1][None], rows_per_mod=seq, tm=512, fc=256)

    u, bg = _conv_in(x2, mod1, conv_w_in[0].astype(BF16), rows_per_mod=seq, tm=512)
    cw = jnp.concatenate([conv_w[0], jnp.zeros((8 - CONV_WIDTH, d), F32)], axis=0)
    wr = jnp.concatenate([moe_router[0], jnp.zeros((d, LANES - N_EXPERTS), F32)], axis=1).astype(BF16)
    x3, h4, rt = _conv_out(u.reshape(bsz, seq, d), bg.reshape(bsz, seq, d), cw, conv_w_out[0].astype(BF16),
                           x2.reshape(bsz, seq, d), mod1, ln_g[1, 0][None], ln_b[1, 0][None], wr, tm=512)
    rt = rt.reshape(t, LANES)
    plan, counts = _moe_plan(rt, tp=512)
    dest1, dest2, pad_lo, pad_hi, tile_expert, n_active, n_rows = _moe_layout(plan, counts, tile=MOE_TILE)
    xs = _moe_dispatch(h4, dest1, dest2, pad_lo, pad_hi, n_active, n_rows=n_rows, tm=512, tile=MOE_TILE)
    ys = _moe_experts(xs, tile_expert, n_active, moe_w1[0].astype(BF16), moe_w3[0].astype(BF16),
                      moe_w2[0].astype(BF16), tm=MOE_TILE, fc=512)
    out = _moe_combine_ln(ys, dest1, dest2, rt, x3.reshape(t, d), mod1, ln_g[1, 1][None], ln_b[1, 1][None],
                          rows_per_mod=seq, tm=512)
    return out.reshape(bsz, seq, d)
```

```python
import functools

import jax
import jax.numpy as jnp
from jax import lax
from jax.experimental import pallas as pl
from jax.experimental.pallas import tpu as pltpu

F32 = jnp.float32
BF16 = jnp.bfloat16

GRID_W = 64
HEAD_DIM = 64
GQA_GROUP = 4
WINDOW = 128
ROPE_THETA = 10000.0
CONV_WIDTH = 3
N_EXPERTS = 8
LN_EPS = 1e-5
MASK_VALUE = -1e30
DEPTH = 2
DEEPNORM_ALPHA = (2 * DEPTH) ** 0.25
LOG2_E = 1.4426950408889634

LANES = 128
MOD_ROWS = 8
VMEM_LIMIT = 56 * 1024 * 1024


def _cparams(sem):
    return pltpu.CompilerParams(dimension_semantics=sem, vmem_limit_bytes=VMEM_LIMIT)


def _const_spec(shape):
    nd = len(shape)
    return pl.BlockSpec(shape, lambda *_: (0,) * nd, pipeline_mode=pl.Buffered(1))


def _silu(a):
    return a / (1.0 + jnp.exp(-a))


SUBLANES = 8


def _store_token_tiles(ref, val):
    for j in range(SUBLANES):
        ref[:, j, :] = val[:, j * LANES:(j + 1) * LANES]


def _load_token_tiles(ref):
    return jnp.concatenate([ref[:, j, :] for j in range(SUBLANES)], axis=-1)


def _deepnorm_ln(x, mix, gate, g, b):
    y = DEEPNORM_ALPHA * x + gate * mix
    mu = jnp.mean(y, axis=-1, keepdims=True)
    d = y - mu
    var = jnp.mean(d * d, axis=-1, keepdims=True)
    return d * lax.rsqrt(var + LN_EPS) * g + b


def _mod_kernel(c_ref, w_ref, b_ref, o_ref):
    cv = c_ref[...]
    s = _silu(cv).astype(BF16)
    o_ref[0] = jnp.dot(s, w_ref[0].astype(BF16), preferred_element_type=F32) + b_ref[0]


def _modulation(cvec, w_mod, b_mod):
    depth, d, n = w_mod.shape
    rows = cvec.shape[0]
    nt = 1536
    return pl.pallas_call(
        _mod_kernel,
        grid=(depth, n // nt),
        in_specs=[
            pl.BlockSpec((rows, d), lambda l, j: (0, 0)),
            pl.BlockSpec((1, d, nt), lambda l, j: (l, 0, j)),
            pl.BlockSpec((1, 1, nt), lambda l, j: (l, 0, j)),
        ],
        out_specs=pl.BlockSpec((1, rows, nt), lambda l, j: (l, 0, j)),
        out_shape=jax.ShapeDtypeStruct((depth, rows, n), F32),
        compiler_params=_cparams(("parallel", "parallel")),
        name="modulation",
    )(cvec, w_mod, b_mod.reshape(depth, 1, n))


def _proj_kernel(x_ref, mod_ref, w_ref, cos_ref, sin_ref, *o_refs, rope_lanes, q_lanes, splits):
    m = mod_ref[0]
    h = x_ref[...] * (1.0 + m[1:2, :]) + m[0:1, :]
    y = jnp.dot(h.astype(BF16), w_ref[...], preferred_element_type=F32)
    if rope_lanes:
        cos = cos_ref[...]
        sin = sin_ref[...]
        first_half = (lax.broadcasted_iota(jnp.int32, (1, LANES), 1) % 32) < 16
    off = 0
    for o_ref, width in zip(o_refs, splits):
        for c0 in range(0, width, LANES):
            yc = y[:, off + c0:off + c0 + LANES]
            if off + c0 < rope_lanes:
                rot = jnp.where(first_half, pltpu.roll(yc, LANES - 16, 1), pltpu.roll(yc, 16, 1))
                yc = yc * cos + rot * sin
                if off + c0 < q_lanes:
                    yc = yc * (HEAD_DIM ** -0.5 * LOG2_E)
            o_ref[:, c0:c0 + LANES] = yc.astype(o_ref.dtype)
        off += width


def _project(x, mod, w, cos, sin, *, rows_per_mod, tm, rope_lanes, q_lanes, splits, name):
    t, d = x.shape
    n = w.shape[1]
    steps_per_mod = rows_per_mod // tm
    steps_per_seq = cos.shape[0] // tm
    kern = functools.partial(_proj_kernel, rope_lanes=rope_lanes, q_lanes=q_lanes, splits=splits)
    return pl.pallas_call(
        kern,
        grid=(t // tm,),
        in_specs=[
            pl.BlockSpec((tm, d), lambda i: (i, 0)),
            pl.BlockSpec((1, MOD_ROWS, d), lambda i: (i // steps_per_mod, 0, 0)),
            _const_spec((d, n)),
            pl.BlockSpec((tm, LANES), lambda i: (i % steps_per_seq, 0)),
            pl.BlockSpec((tm, LANES), lambda i: (i % steps_per_seq, 0)),
        ],
        out_specs=[pl.BlockSpec((tm, s), lambda i: (i, 0)) for s in splits],
        out_shape=[jax.ShapeDtypeStruct((t, s), BF16) for s in splits],
        compiler_params=_cparams(("parallel",)),
        name=name,
    )(x, mod, w, cos, sin)


def _attn_kernel(sink_ref, q_ref, km_ref, kp_ref, kn_ref, vm_ref, vp_ref, vn_ref, kvc_ref, o_ref, *, tq, seq, n_ctx):
    i = pl.program_id(1)
    nsub = tq // WINDOW
    span = 3 * WINDOW
    nkv = km_ref.shape[2] // LANES
    lane = lax.broadcasted_iota(jnp.int32, (1, LANES), 1)
    lo = lane < HEAD_DIM
    r = lax.broadcasted_iota(jnp.int32, (WINDOW, WINDOW), 0)
    c = lax.broadcasted_iota(jnp.int32, (WINDOW, WINDOW), 1)
    valid_first, valid_last = [], []
    for j in range(nsub):
        blk = i * (tq // WINDOW) + j
        valid_first.append((c >= r) & (blk > 0))
        valid_last.append((c <= r) & (blk < seq // WINDOW - 1))
    gidx = lax.broadcasted_iota(jnp.int32, (GQA_GROUP, 1, 1), 0)
    zero = jnp.zeros((), BF16)
    for kh in range(nkv):
        ks = slice(kh * LANES, (kh + 1) * LANES)
        kall = jnp.concatenate([kp_ref[0, :, ks], km_ref[0, :, ks], kn_ref[0, :, ks]], axis=0)
        vall = jnp.concatenate([vp_ref[0, :, ks], vm_ref[0, :, ks], vn_ref[0, :, ks]], axis=0)
        kc = kvc_ref[0, :, ks]
        vc = kvc_ref[0, :, nkv * LANES + kh * LANES:nkv * LANES + (kh + 1) * LANES]
        sink = jnp.full((GQA_GROUP, 1, 1), sink_ref[kh * GQA_GROUP], F32)
        for g in range(1, GQA_GROUP):
            sink = jnp.where(gidx == g, sink_ref[kh * GQA_GROUP + g], sink)
        sink = sink * LOG2_E
        for j in range(nsub):
            rows = slice(j * WINDOW, (j + 1) * WINDOW)
            kk = jnp.concatenate([kc, kall[j * WINDOW:j * WINDOW + span]], axis=0)
            vv = jnp.concatenate([vc, vall[j * WINDOW:j * WINDOW + span]], axis=0)
            parts = []
            for t in range(GQA_GROUP // 2):
                q2 = q_ref[0, rows, (2 * kh + t) * LANES:(2 * kh + t + 1) * LANES]
                parts += [jnp.where(lo, q2, zero), jnp.where(lo, zero, q2)]
            lhs = jnp.concatenate(parts, axis=0)
            s = lax.dot_general(lhs, kk, (((1,), (1,)), ((), ())), preferred_element_type=F32)
            s = s.reshape(GQA_GROUP, WINDOW, n_ctx + span)
            s = jnp.concatenate([
                s[:, :, :n_ctx],
                jnp.where(valid_first[j][None], s[:, :, n_ctx:n_ctx + WINDOW], MASK_VALUE),
                s[:, :, n_ctx + WINDOW:n_ctx + 2 * WINDOW],
                jnp.where(valid_last[j][None], s[:, :, n_ctx + 2 * WINDOW:], MASK_VALUE),
            ], axis=-1)
            m = jnp.maximum(jnp.max(s, axis=-1, keepdims=True), sink)
            p = jnp.exp2(s - m)
            denom = jnp.sum(p, axis=-1, keepdims=True) + jnp.exp2(sink - m)
            o = jnp.dot(p.astype(BF16).reshape(GQA_GROUP * WINDOW, n_ctx + span), vv, preferred_element_type=F32)
            o = o.reshape(GQA_GROUP, WINDOW, LANES) / denom
            for t in range(GQA_GROUP // 2):
                o2 = jnp.where(lo, o[2 * t], o[2 * t + 1])
                o_ref[0, rows, (2 * kh + t) * LANES:(2 * kh + t + 1) * LANES] = o2.astype(o_ref.dtype)


def _attention(q, kd, vd, kvc, sink, *, tq):
    b, s, qd = q.shape
    kvd = kd.shape[2]
    n_ctx = kvc.shape[1]
    nblk = s // WINDOW
    per = tq // WINDOW
    kern = functools.partial(_attn_kernel, tq=tq, seq=s, n_ctx=n_ctx)
    main = lambda bi, i: (bi, i, 0)
    prev = lambda bi, i: (bi, jnp.maximum(i * per - 1, 0), 0)
    nxt = lambda bi, i: (bi, jnp.minimum((i + 1) * per, nblk - 1), 0)
    return pl.pallas_call(
        kern,
        grid=(b, s // tq),
        in_specs=[
            pl.BlockSpec(memory_space=pltpu.SMEM),
            pl.BlockSpec((1, tq, qd), main),
            pl.BlockSpec((1, tq, kvd), main),
            pl.BlockSpec((1, WINDOW, kvd), prev),
            pl.BlockSpec((1, WINDOW, kvd), nxt),
            pl.BlockSpec((1, tq, kvd), main),
            pl.BlockSpec((1, WINDOW, kvd), prev),
            pl.BlockSpec((1, WINDOW, kvd), nxt),
            pl.BlockSpec((1, n_ctx, 2 * kvd), lambda bi, i: (bi, 0, 0)),
        ],
        out_specs=pl.BlockSpec((1, tq, qd), main),
        out_shape=jax.ShapeDtypeStruct((b, s, qd), BF16),
        compiler_params=_cparams(("parallel", "parallel")),
        name="window_attention",
    )(sink, q, kd, kd, kd, vd, vd, vd, kvc)


def _tail0_kernel(a_ref, x_ref, mod0_ref, mod1_ref, wo_ref, w1_ref, w3_ref, w2_ref, win_ref, ln_ref,
                  x_out, u_out, bg_out, *, fc):
    m0 = mod0_ref[0]
    m1 = mod1_ref[0]
    ln = ln_ref[...]
    d = x_ref.shape[1]
    mix = jnp.dot(a_ref[...], wo_ref[...], preferred_element_type=F32)
    x1 = _deepnorm_ln(x_ref[...], mix, m0[2:3, :], ln[0:1, :], ln[1:2, :])
    hb = (x1 * (1.0 + m0[4:5, :]) + m0[3:4, :]).astype(BF16)
    acc = jnp.zeros(x1.shape, F32)
    for c0 in range(0, w1_ref.shape[1], fc):
        a = jnp.dot(hb, w1_ref[:, c0:c0 + fc], preferred_element_type=F32)
        bb = jnp.dot(hb, w3_ref[:, c0:c0 + fc], preferred_element_type=F32)
        gact = (_silu(a) * bb).astype(BF16)
        acc = acc + jnp.dot(gact, w2_ref[c0:c0 + fc, :], preferred_element_type=F32)
    x2 = _deepnorm_ln(x1, acc, m0[5:6, :], ln[2:3, :], ln[3:4, :])
    x_out[...] = x2
    hb = (x2 * (1.0 + m1[1:2, :]) + m1[0:1, :]).astype(BF16)
    bg_out[...] = jnp.dot(hb, win_ref[:, :d], preferred_element_type=F32).astype(bg_out.dtype)
    cg = jnp.dot(hb, win_ref[:, d:2 * d], preferred_element_type=F32)
    val = jnp.dot(hb, win_ref[:, 2 * d:], preferred_element_type=F32)
    u_out[...] = (cg * val).astype(u_out.dtype)


def _layer0_tail(a, x, mod0, mod1, wo, w1, w3, w2, win, ln, *, rows_per_mod, tm, fc):
    t, d = x.shape
    spm = rows_per_mod // tm
    row = lambda i: (i, 0)
    modi = lambda i: (i // spm, 0, 0)
    return pl.pallas_call(
        functools.partial(_tail0_kernel, fc=fc),
        grid=(t // tm,),
        in_specs=[
            pl.BlockSpec((tm, a.shape[1]), row),
            pl.BlockSpec((tm, d), row),
            pl.BlockSpec((1, MOD_ROWS, d), modi),
            pl.BlockSpec((1, MOD_ROWS, d), modi),
            _const_spec(wo.shape),
            _const_spec(w1.shape),
            _const_spec(w3.shape),
            _const_spec(w2.shape),
            _const_spec(win.shape),
            _const_spec(ln.shape),
        ],
        out_specs=[pl.BlockSpec((tm, d), row)] * 3,
        out_shape=[jax.ShapeDtypeStruct((t, d), F32), jax.ShapeDtypeStruct((t, d), BF16),
                   jax.ShapeDtypeStruct((t, d), BF16)],
        compiler_params=_cparams(("parallel",)),
        name="oproj_ffn_convin",
    )(a, x, mod0, mod1, wo, w1, w3, w2, win, ln)


HALO = 16
RT_G1, RT_G2, RT_E1, RT_E2 = 8, 9, 10, 11


def _convout_kernel(u_ref, up_ref, un_ref, bg_ref, cw_ref, w_ref, x_ref, mod_ref, g_ref, b_ref, wr_ref,
                    x_out, h_out, route_out):
    j = pl.program_id(1)
    nj = pl.num_programs(1)
    m = mod_ref[0]
    u = u_ref[0].astype(F32)
    tm = u.shape[0]
    row = lax.broadcasted_iota(jnp.int32, (tm, 1), 0)
    prev_row = jnp.where(j == 0, 0.0, up_ref[0, HALO - 1:HALO, :].astype(F32))
    next_row = jnp.where(j == nj - 1, 0.0, un_ref[0, 0:1, :].astype(F32))
    u_prev = jnp.where(row == 0, prev_row, pltpu.roll(u, 1, 0))
    u_next = jnp.where(row == tm - 1, next_row, pltpu.roll(u, tm - 1, 0))
    cw = cw_ref[...]
    y = cw[0:1, :] * u_prev + cw[1:2, :] * u + cw[2:3, :] * u_next
    z = (bg_ref[0].astype(F32) * y).astype(BF16)
    mix = jnp.dot(z, w_ref[...], preferred_element_type=F32)
    x_new = _deepnorm_ln(x_ref[0], mix, m[2:3, :], g_ref[...], b_ref[...])
    x_out[0] = x_new
    h = x_new * (1.0 + m[4:5, :]) + m[3:4, :]
    for c in range(SUBLANES):
        h_out[0, c] = h[:, c * LANES:(c + 1) * LANES]
    hb = h.astype(BF16)
    lg = jnp.dot(hb, wr_ref[...], preferred_element_type=F32)
    lane = lax.broadcasted_iota(jnp.int32, lg.shape, 1).astype(F32)
    neg = jnp.float32(-jnp.inf)
    lg = jnp.where(lane < N_EXPERTS, lg, neg)
    m1 = jnp.max(lg, axis=-1, keepdims=True)
    i1 = jnp.min(jnp.where(lg == m1, lane, float(LANES)), axis=-1, keepdims=True)
    lg2 = jnp.where(lane == i1, neg, lg)
    m2 = jnp.max(lg2, axis=-1, keepdims=True)
    i2 = jnp.min(jnp.where(lg2 == m2, lane, float(LANES)), axis=-1, keepdims=True)
    e2 = jnp.exp(m2 - m1)
    gate1 = 1.0 / (1.0 + e2)
    gate2 = e2 / (1.0 + e2)
    member = jnp.where((lane == i1) | (lane == i2), 1.0, 0.0)
    extra = jnp.where(lane == RT_G1, gate1, jnp.where(lane == RT_G2, gate2,
                      jnp.where(lane == RT_E1, i1, jnp.where(lane == RT_E2, i2, 0.0))))
    route_out[0] = member + extra


def _conv_out(u, bg, cw, w, x, mod, g, b, wr, *, tm):
    bsz, s, d = x.shape
    per = tm // HALO
    nh = s // HALO
    main = lambda bi, j: (bi, j, 0)
    return pl.pallas_call(
        _convout_kernel,
        grid=(bsz, s // tm),
        in_specs=[
            pl.BlockSpec((1, tm, d), main),
            pl.BlockSpec((1, HALO, d), lambda bi, j: (bi, jnp.maximum(j * per - 1, 0), 0)),
            pl.BlockSpec((1, HALO, d), lambda bi, j: (bi, jnp.minimum((j + 1) * per, nh - 1), 0)),
            pl.BlockSpec((1, tm, d), main),
            _const_spec(cw.shape),
            _const_spec(w.shape),
            pl.BlockSpec((1, tm, d), main),
            pl.BlockSpec((1, MOD_ROWS, d), lambda bi, j: (bi, 0, 0)),
            _const_spec((1, d)),
            _const_spec((1, d)),
            _const_spec(wr.shape),
        ],
        out_specs=[
            pl.BlockSpec((1, tm, d), main),
            pl.BlockSpec((1, SUBLANES, tm, d // SUBLANES), lambda bi, j: (bi, 0, j, 0)),
            pl.BlockSpec((1, tm, LANES), main),
        ],
        out_shape=[
            jax.ShapeDtypeStruct((bsz, s, d), F32),
            jax.ShapeDtypeStruct((bsz, SUBLANES, s, d // SUBLANES), F32),
            jax.ShapeDtypeStruct((bsz, s, LANES), F32),
        ],
        compiler_params=_cparams(("parallel", "parallel")),
        name="conv_out_ln_router",
    )(u, u, u, bg, cw, w, x, mod, g, b, wr)


PLAN_ROWS = 8
MOE_TILE = 512


def _plan_kernel(rt_ref, plan_ref, cnt_ref, carry_ref):
    i = pl.program_id(0)

    @pl.when(i == 0)
    def _():
        carry_ref[...] = jnp.zeros_like(carry_ref)

    rt = rt_ref[...]
    tp = rt.shape[0]
    lane = lax.broadcasted_iota(jnp.int32, rt.shape, 1)
    lane_f = lane.astype(F32)
    member = jnp.where(lane < N_EXPERTS, rt, 0.0)
    r = lax.broadcasted_iota(jnp.int32, (tp, tp), 0)
    c = lax.broadcasted_iota(jnp.int32, (tp, tp), 1)
    tri = jnp.where(c <= r, 1.0, 0.0).astype(BF16)
    incl = jnp.dot(tri, member.astype(BF16), preferred_element_type=F32)
    before = incl - member + carry_ref[0:1, :]
    e1 = jnp.sum(jnp.where(lane == RT_E1, rt, 0.0), axis=-1, keepdims=True)
    e2 = jnp.sum(jnp.where(lane == RT_E2, rt, 0.0), axis=-1, keepdims=True)
    rank1 = jnp.sum(jnp.where(lane_f == e1, before, 0.0), axis=-1, keepdims=True)
    rank2 = jnp.sum(jnp.where(lane_f == e2, before, 0.0), axis=-1, keepdims=True)
    rec = jnp.where(lane == 0, rank1, jnp.where(lane == 1, rank2,
                    jnp.where(lane == 2, e1, jnp.where(lane == 3, e2, 0.0))))
    plan_ref[...] = rec.T[:PLAN_ROWS, :].astype(jnp.int32)
    carry_ref[0:1, :] = carry_ref[0:1, :] + incl[tp - 1:tp, :]
    cnt_ref[...] = carry_ref[...]


def _moe_plan(rt, *, tp):
    t = rt.shape[0]
    return pl.pallas_call(
        _plan_kernel,
        grid=(t // tp,),
        in_specs=[pl.BlockSpec((tp, LANES), lambda i: (i, 0))],
        out_specs=[
            pl.BlockSpec((PLAN_ROWS, tp), lambda i: (0, i)),
            pl.BlockSpec((8, LANES), lambda i: (0, 0)),
        ],
        out_shape=[
            jax.ShapeDtypeStruct((PLAN_ROWS, t), jnp.int32),
            jax.ShapeDtypeStruct((8, LANES), F32),
        ],
        scratch_shapes=[pltpu.VMEM((8, LANES), F32)],
        compiler_params=_cparams(("arbitrary",)),
        name="moe_plan",
    )(rt)


def _row_copy(src_ref, src_row, dst_ref, dst_row, sem):
    return pltpu.make_async_copy(src_ref.at[src_row], dst_ref.at[dst_row], sem)


def _chunked_row(ref, row):
    return ref.at[:, row, :]


DMA_UNROLL = 8


def _dispatch_kernel(d1_ref, d2_ref, padlo_ref, padhi_ref, na_ref, h_ref, xs_ref, zero_ref, sem, sem2, zsem, *, tile):
    i = pl.program_id(0)
    tm = h_ref.shape[2]
    hsrc = h_ref.at[0]
    n_tiles = xs_ref.shape[0] // tile

    @pl.when(i == 0)
    def _():
        zero_ref[...] = jnp.zeros_like(zero_ref)

        def zrow_start(r, carry):
            _row_copy(zero_ref, 0, xs_ref, r, zsem).start()
            return carry

        def zrow_wait(r, carry):
            _row_copy(zero_ref, 0, xs_ref, r, zsem).wait()
            return carry

        def ztile_start(k, carry):
            pltpu.make_async_copy(zero_ref, xs_ref.at[pl.ds(k * tile, tile)], zsem).start()
            return carry

        def ztile_wait(k, carry):
            pltpu.make_async_copy(zero_ref, xs_ref.at[pl.ds(k * tile, tile)], zsem).wait()
            return carry

        for e in range(N_EXPERTS):
            lax.fori_loop(padlo_ref[e], padhi_ref[e], zrow_start, 0)
            lax.fori_loop(padlo_ref[e], padhi_ref[e], zrow_wait, 0)
        lax.fori_loop(na_ref[0], n_tiles, ztile_start, 0)
        lax.fori_loop(na_ref[0], n_tiles, ztile_wait, 0)

    base = i * tm

    def start(r, carry):
        pltpu.make_async_copy(_chunked_row(hsrc, r), xs_ref.at[d1_ref[base + r]], sem).start(priority=0)
        pltpu.make_async_copy(_chunked_row(hsrc, r), xs_ref.at[d2_ref[base + r]], sem2).start(priority=1)
        return carry

    def wait(r, carry):
        pltpu.make_async_copy(_chunked_row(hsrc, r), xs_ref.at[d1_ref[base + r]], sem).wait()
        pltpu.make_async_copy(_chunked_row(hsrc, r), xs_ref.at[d2_ref[base + r]], sem2).wait()
        return carry

    lax.fori_loop(0, tm, start, 0, unroll=DMA_UNROLL)
    lax.fori_loop(0, tm, wait, 0, unroll=DMA_UNROLL)


def _moe_dispatch(h, dest1, dest2, pad_lo, pad_hi, n_active, *, n_rows, tm, tile):
    bsz, nch, seq, lanes = h.shape
    per = seq // tm
    smem = pl.BlockSpec(memory_space=pltpu.SMEM)
    dma = pltpu.SemaphoreType.DMA(())
    return pl.pallas_call(
        functools.partial(_dispatch_kernel, tile=tile),
        grid=(bsz * per,),
        in_specs=[smem, smem, smem, smem, smem,
                  pl.BlockSpec((1, nch, tm, lanes), lambda i: (i // per, 0, i % per, 0))],
        out_specs=pl.BlockSpec(memory_space=pl.ANY),
        out_shape=jax.ShapeDtypeStruct((n_rows, nch, lanes), F32),
        scratch_shapes=[pltpu.VMEM((tile, nch, lanes), F32), dma, dma, dma],
        compiler_params=_cparams(("arbitrary",)),
        name="moe_dispatch",
    )(dest1, dest2, pad_lo, pad_hi, n_active, h)


def _expert_kernel(te_ref, na_ref, x_ref, w1_ref, w3_ref, w2_ref, o_ref, *, fc):
    i = pl.program_id(0)

    @pl.when(i < na_ref[0])
    def _():
        xb = _load_token_tiles(x_ref).astype(BF16)
        acc = jnp.zeros(xb.shape, F32)
        for c0 in range(0, w1_ref.shape[2], fc):
            a = jnp.dot(xb, w1_ref[0, :, c0:c0 + fc], preferred_element_type=F32)
            bb = jnp.dot(xb, w3_ref[0, :, c0:c0 + fc], preferred_element_type=F32)
            gact = (_silu(a) * bb).astype(BF16)
            acc = acc + jnp.dot(gact, w2_ref[0, c0:c0 + fc, :], preferred_element_type=F32)
        _store_token_tiles(o_ref, acc)

    @pl.when(i >= na_ref[0])
    def _():
        o_ref[...] = jnp.zeros_like(o_ref)


def _moe_experts(xs, tile_expert, n_active, w1, w3, w2, *, tm, fc):
    n_rows = xs.shape[0]
    row = xs.shape[1:]
    _, d, ff = w1.shape

    def wspec(shape):
        return pl.BlockSpec(shape, lambda i, te, na: (te[i], 0, 0), pipeline_mode=pl.Buffered(1))

    grid_spec = pltpu.PrefetchScalarGridSpec(
        num_scalar_prefetch=2,
        grid=(n_rows // tm,),
        in_specs=[
            pl.BlockSpec((tm,) + row, lambda i, te, na: (jnp.minimum(i, na[0] - 1), 0, 0)),
            wspec((1, d, ff)),
            wspec((1, d, ff)),
            wspec((1, ff, d)),
        ],
        out_specs=pl.BlockSpec((tm,) + row, lambda i, te, na: (i, 0, 0)),
    )
    return pl.pallas_call(
        functools.partial(_expert_kernel, fc=fc),
        grid_spec=grid_spec,
        out_shape=jax.ShapeDtypeStruct((n_rows,) + row, F32),
        compiler_params=_cparams(("arbitrary",)),
        name="moe_experts",
    )(tile_expert, n_active, xs, w1, w3, w2)


def _combine_kernel(d1_ref, d2_ref, ys_ref, rt_ref, x_ref, mod_ref, g_ref, b_ref, o_ref, y1_ref, y2_ref, sem, sem2):
    i = pl.program_id(0)
    tm = x_ref.shape[0]
    base = i * tm

    def start(r, carry):
        pltpu.make_async_copy(ys_ref.at[d1_ref[base + r]], _chunked_row(y1_ref, r), sem).start(priority=0)
        pltpu.make_async_copy(ys_ref.at[d2_ref[base + r]], _chunked_row(y2_ref, r), sem2).start(priority=1)
        return carry

    def wait(r, carry):
        pltpu.make_async_copy(ys_ref.at[d1_ref[base + r]], _chunked_row(y1_ref, r), sem).wait()
        pltpu.make_async_copy(ys_ref.at[d2_ref[base + r]], _chunked_row(y2_ref, r), sem2).wait()
        return carry

    lax.fori_loop(0, tm, start, 0, unroll=DMA_UNROLL)
    lax.fori_loop(0, tm, wait, 0, unroll=DMA_UNROLL)
    rt = rt_ref[...]
    y1 = jnp.concatenate([y1_ref[c] for c in range(SUBLANES)], axis=-1)
    y2 = jnp.concatenate([y2_ref[c] for c in range(SUBLANES)], axis=-1)
    y = rt[:, RT_G1:RT_G1 + 1] * y1 + rt[:, RT_G2:RT_G2 + 1] * y2
    o_ref[...] = _deepnorm_ln(x_ref[...], y, mod_ref[0][5:6, :], g_ref[...], b_ref[...])


def _moe_combine_ln(ys, dest1, dest2, rt, x, mod, g, b, *, rows_per_mod, tm):
    t, d = x.shape
    row = ys.shape[1:]
    spm = rows_per_mod // tm
    smem = pl.BlockSpec(memory_space=pltpu.SMEM)
    dma = pltpu.SemaphoreType.DMA(())
    return pl.pallas_call(
        _combine_kernel,
        grid=(t // tm,),
        in_specs=[
            smem, smem,
            pl.BlockSpec(memory_space=pl.ANY),
            pl.BlockSpec((tm, LANES), lambda i: (i, 0)),
            pl.BlockSpec((tm, d), lambda i: (i, 0)),
            pl.BlockSpec((1, MOD_ROWS, d), lambda i: (i // spm, 0, 0)),
            _const_spec((1, d)),
            _const_spec((1, d)),
        ],
        out_specs=pl.BlockSpec((tm, d), lambda i: (i, 0)),
        out_shape=jax.ShapeDtypeStruct((t, d), F32),
        scratch_shapes=[pltpu.VMEM((row[0], tm, row[1]), F32), pltpu.VMEM((row[0], tm, row[1]), F32), dma, dma],
        compiler_params=_cparams(("arbitrary",)),
        name="moe_combine_ln",
    )(dest1, dest2, ys, rt, x, mod, g, b)


def _moe_layout(plan, counts, *, tile):
    n_pairs = plan.shape[1] * 2
    n_tiles = n_pairs // tile + N_EXPERTS
    cnt = counts[0, :N_EXPERTS].astype(jnp.int32)
    padded = (cnt + tile - 1) // tile * tile
    ends = jnp.cumsum(padded)
    starts = ends - padded
    dest1 = starts[plan[2]] + plan[0]
    dest2 = starts[plan[3]] + plan[1]
    n_active = ends[-1:] // tile
    tiles = jnp.minimum(jnp.arange(n_tiles, dtype=jnp.int32), n_active[0] - 1) * tile
    tile_expert = jnp.minimum(jnp.searchsorted(ends, tiles, side="right"), N_EXPERTS - 1).astype(jnp.int32)
    return dest1, dest2, starts + cnt, ends, tile_expert, n_active.astype(jnp.int32), n_tiles * tile


def _rope_tables(rows_count):
    half = HEAD_DIM // 2
    rows = jnp.repeat(jnp.arange(rows_count, dtype=F32), GRID_W)
    cols = jnp.tile(jnp.arange(GRID_W, dtype=F32), rows_count)
    inv_freq = ROPE_THETA ** (-jnp.arange(0, half, 2, dtype=F32) / half)
    ar = rows[:, None] * inv_freq
    ac = cols[:, None] * inv_freq
    ang = jnp.concatenate([ar, ar, ac, ac], axis=-1)
    sign = jnp.where((jnp.arange(HEAD_DIM) % 32) < 16, -1.0, 1.0).astype(F32)
    cos = jnp.tile(jnp.cos(ang), (1, LANES // HEAD_DIM))
    sin = jnp.tile(jnp.sin(ang) * sign, (1, LANES // HEAD_DIM))
    return cos, sin


def _dup_heads(w, n_heads):
    d = w.shape[0]
    w = w.reshape(d, n_heads, 1, HEAD_DIM)
    return jnp.broadcast_to(w, (d, n_heads, 2, HEAD_DIM)).reshape(d, n_heads * 2 * HEAD_DIM)


def _mod_blocks(mod_rows, d):
    r = mod_rows.shape[0]
    m = mod_rows.reshape(r, 6, d)
    return jnp.concatenate([m, jnp.zeros((r, MOD_ROWS - 6, d), F32)], axis=1)


def kernel(x, c, ctx, c_ctx, w_mod, b_mod, ln_g, ln_b, attn_w_qkv, attn_w_o, attn_sink,
           conv_w_in, conv_w, conv_w_out, ffn_w1, ffn_w3, ffn_w2,
           moe_router, moe_w1, moe_w3, moe_w2):
    bsz, seq, d = x.shape
    n_ctx = ctx.shape[1]
    t = bsz * seq
    q_dim = attn_w_o.shape[1]
    kv_dim = (attn_w_qkv.shape[2] - q_dim) // 2
    n_kv = kv_dim // HEAD_DIM

    pad_rows = (-(bsz + 1)) % 8
    cvec = jnp.concatenate([c, c_ctx[None, :], jnp.zeros((pad_rows, d), F32)], axis=0)
    mod_all = _modulation(cvec, w_mod, b_mod)
    mod0 = _mod_blocks(mod_all[0, :bsz], d)
    modc0 = _mod_blocks(mod_all[0, bsz:bsz + 1], d)
    mod1 = _mod_blocks(mod_all[1, :bsz], d)

    cos, sin = _rope_tables(seq // GRID_W)

    wqkv = attn_w_qkv[0]
    wk = _dup_heads(wqkv[:, q_dim:q_dim + kv_dim], n_kv)
    wv = _dup_heads(wqkv[:, q_dim + kv_dim:], n_kv)
    w_all = jnp.concatenate([wqkv[:, :q_dim], wk, wv], axis=1).astype(BF16)
    w_kv = jnp.concatenate([wk, wv], axis=1).astype(BF16)
    x2d = x.reshape(t, d)
    q, kd, vd = _project(x2d, mod0, w_all, cos, sin, rows_per_mod=seq, tm=512,
                         rope_lanes=q_dim + 2 * kv_dim, q_lanes=q_dim,
                         splits=(q_dim, 2 * kv_dim, 2 * kv_dim), name="qkv_rope")
    (kvc,) = _project(ctx.reshape(bsz * n_ctx, d), modc0, w_kv, cos, sin, rows_per_mod=bsz * n_ctx, tm=512,
                      rope_lanes=0, q_lanes=0, splits=(4 * kv_dim,), name="ctx_kv")
    attn = _attention(q.reshape(bsz, seq, q_dim), kd.reshape(bsz, seq, 2 * kv_dim), vd.reshape(bsz, seq, 2 * kv_dim),
                      kvc.reshape(bsz, n_ctx, 4 * kv_dim), attn_sink[0], tq=256)
    ln0 = jnp.concatenate([ln_g[0, 0][None], ln_b[0, 0][None], ln_g[0, 1][None], ln_b[0, 1][None],
                           jnp.zeros((4, d), F32)], axis=0)
    x2, u, bg = _layer0_tail(attn.reshape(t, q_dim), x2d, mod0, mod1, attn_w_o[0].astype(BF16),
                             ffn_w1[0].astype(BF16), ffn_w3[0].astype(BF16), ffn_w2[0].astype(BF16),
                             conv_w_in[0].astype(BF16), ln0, rows_per_mod=seq, tm=512, fc=256)

    cw = jnp.concatenate([conv_w[0], jnp.zeros((8 - CONV_WIDTH, d), F32)], axis=0)
    wr = jnp.concatenate([moe_router[0], jnp.zeros((d, LANES - N_EXPERTS), F32)], axis=1).astype(BF16)
    x3, h4, rt = _conv_out(u.reshape(bsz, seq, d), bg.reshape(bsz, seq, d), cw, conv_w_out[0].astype(BF16),
                           x2.reshape(bsz, seq, d), mod1, ln_g[1, 0][None], ln_b[1, 0][None], wr, tm=512)
    rt = rt.reshape(t, LANES)
    plan, counts = _moe_plan(rt, tp=512)
    dest1, dest2, pad_lo, pad_hi, tile_expert, n_active, n_rows = _moe_layout(plan, counts, tile=MOE_TILE)
    xs = _moe_dispatch(h4, dest1, dest2, pad_lo, pad_hi, n_active, n_rows=n_rows, tm=512, tile=MOE_TILE)
    ys = _moe_experts(xs, tile_expert, n_active, moe_w1[0].astype(BF16), moe_w3[0].astype(BF16),
                      moe_w2[0].astype(BF16), tm=MOE_TILE, fc=512)
    out = _moe_combine_ln(ys, dest1, dest2, rt, x3.reshape(t, d), mod1, ln_g[1, 1][None], ln_b[1, 1][None],
                          rows_per_mod=seq, tm=512)
    return out.reshape(bsz, seq, d)
```

```python
import functools

import jax
import jax.numpy as jnp
from jax import lax
from jax.experimental import pallas as pl
from jax.experimental.pallas import tpu as pltpu

F32 = jnp.float32
BF16 = jnp.bfloat16

GRID_W = 64
HEAD_DIM = 64
GQA_GROUP = 4
WINDOW = 128
ROPE_THETA = 10000.0
CONV_WIDTH = 3
N_EXPERTS = 8
LN_EPS = 1e-5
MASK_VALUE = -1e30
DEPTH = 2
DEEPNORM_ALPHA = (2 * DEPTH) ** 0.25
LOG2_E = 1.4426950408889634

LANES = 128
MOD_ROWS = 8
VMEM_LIMIT = 56 * 1024 * 1024


def _cparams(sem):
    return pltpu.CompilerParams(dimension_semantics=sem, vmem_limit_bytes=VMEM_LIMIT)


def _const_spec(shape):
    nd = len(shape)
    return pl.BlockSpec(shape, lambda *_: (0,) * nd, pipeline_mode=pl.Buffered(1))


def _silu(a):
    return a / (1.0 + jnp.exp(-a))


def _deepnorm_ln(x, mix, gate, g, b):
    y = DEEPNORM_ALPHA * x + gate * mix
    mu = jnp.mean(y, axis=-1, keepdims=True)
    d = y - mu
    var = jnp.mean(d * d, axis=-1, keepdims=True)
    return d * lax.rsqrt(var + LN_EPS) * g + b


def _mod_kernel(c_ref, w_ref, b_ref, o_ref):
    cv = c_ref[...]
    s = _silu(cv).astype(BF16)
    o_ref[0] = jnp.dot(s, w_ref[0].astype(BF16), preferred_element_type=F32) + b_ref[0]


def _modulation(cvec, w_mod, b_mod):
    depth, d, n = w_mod.shape
    rows = cvec.shape[0]
    nt = 1536
    return pl.pallas_call(
        _mod_kernel,
        grid=(depth, n // nt),
        in_specs=[
            pl.BlockSpec((rows, d), lambda l, j: (0, 0)),
            pl.BlockSpec((1, d, nt), lambda l, j: (l, 0, j)),
            pl.BlockSpec((1, 1, nt), lambda l, j: (l, 0, j)),
        ],
        out_specs=pl.BlockSpec((1, rows, nt), lambda l, j: (l, 0, j)),
        out_shape=jax.ShapeDtypeStruct((depth, rows, n), F32),
        compiler_params=_cparams(("parallel", "parallel")),
        name="modulation",
    )(cvec, w_mod, b_mod.reshape(depth, 1, n))


def _proj_kernel(x_ref, mod_ref, w_ref, cos_ref, sin_ref, *o_refs, rope_lanes, q_lanes, splits):
    m = mod_ref[0]
    h = x_ref[...] * (1.0 + m[1:2, :]) + m[0:1, :]
    y = jnp.dot(h.astype(BF16), w_ref[...], preferred_element_type=F32)
    if rope_lanes:
        cos = cos_ref[...]
        sin = sin_ref[...]
        first_half = (lax.broadcasted_iota(jnp.int32, (1, LANES), 1) % 32) < 16
    off = 0
    for o_ref, width in zip(o_refs, splits):
        for c0 in range(0, width, LANES):
            yc = y[:, off + c0:off + c0 + LANES]
            if off + c0 < rope_lanes:
                rot = jnp.where(first_half, pltpu.roll(yc, LANES - 16, 1), pltpu.roll(yc, 16, 1))
                yc = yc * cos + rot * sin
                if off + c0 < q_lanes:
                    yc = yc * (HEAD_DIM ** -0.5 * LOG2_E)
            o_ref[:, c0:c0 + LANES] = yc.astype(o_ref.dtype)
        off += width


def _project(x, mod, w, cos, sin, *, rows_per_mod, tm, rope_lanes, q_lanes, splits, name):
    t, d = x.shape
    n = w.shape[1]
    steps_per_mod = rows_per_mod // tm
    steps_per_seq = cos.shape[0] // tm
    kern = functools.partial(_proj_kernel, rope_lanes=rope_lanes, q_lanes=q_lanes, splits=splits)
    return pl.pallas_call(
        kern,
        grid=(t // tm,),
        in_specs=[
            pl.BlockSpec((tm, d), lambda i: (i, 0)),
            pl.BlockSpec((1, MOD_ROWS, d), lambda i: (i // steps_per_mod, 0, 0)),
            _const_spec((d, n)),
            pl.BlockSpec((tm, LANES), lambda i: (i % steps_per_seq, 0)),
            pl.BlockSpec((tm, LANES), lambda i: (i % steps_per_seq, 0)),
        ],
        out_specs=[pl.BlockSpec((tm, s), lambda i: (i, 0)) for s in splits],
        out_shape=[jax.ShapeDtypeStruct((t, s), BF16) for s in splits],
        compiler_params=_cparams(("parallel",)),
        name=name,
    )(x, mod, w, cos, sin)


def _attn_kernel(sink_ref, q_ref, km_ref, kp_ref, kn_ref, vm_ref, vp_ref, vn_ref, kvc_ref, o_ref, *, tq, seq, n_ctx):
    i = pl.program_id(1)
    nsub = tq // WINDOW
    span = 3 * WINDOW
    nkv = km_ref.shape[2] // LANES
    lane = lax.broadcasted_iota(jnp.int32, (1, LANES), 1)
    lo = lane < HEAD_DIM
    r = lax.broadcasted_iota(jnp.int32, (WINDOW, WINDOW), 0)
    c = lax.broadcasted_iota(jnp.int32, (WINDOW, WINDOW), 1)
    valid_first, valid_last = [], []
    for j in range(nsub):
        blk = i * (tq // WINDOW) + j
        valid_first.append((c >= r) & (blk > 0))
        valid_last.append((c <= r) & (blk < seq // WINDOW - 1))
    gidx = lax.broadcasted_iota(jnp.int32, (GQA_GROUP, 1, 1), 0)
    zero = jnp.zeros((), BF16)
    for kh in range(nkv):
        ks = slice(kh * LANES, (kh + 1) * LANES)
        kall = jnp.concatenate([kp_ref[0, :, ks], km_ref[0, :, ks], kn_ref[0, :, ks]], axis=0)
        vall = jnp.concatenate([vp_ref[0, :, ks], vm_ref[0, :, ks], vn_ref[0, :, ks]], axis=0)
        kc = kvc_ref[0, :, ks]
        vc = kvc_ref[0, :, nkv * LANES + kh * LANES:nkv * LANES + (kh + 1) * LANES]
        sink = jnp.full((GQA_GROUP, 1, 1), sink_ref[kh * GQA_GROUP], F32)
        for g in range(1, GQA_GROUP):
            sink = jnp.where(gidx == g, sink_ref[kh * GQA_GROUP + g], sink)
        sink = sink * LOG2_E
        for j in range(nsub):
            rows = slice(j * WINDOW, (j + 1) * WINDOW)
            kk = jnp.concatenate([kc, kall[j * WINDOW:j * WINDOW + span]], axis=0)
            vv = jnp.concatenate([vc, vall[j * WINDOW:j * WINDOW + span]], axis=0)
            parts = []
            for t in range(GQA_GROUP // 2):
                q2 = q_ref[0, rows, (2 * kh + t) * LANES:(2 * kh + t + 1) * LANES]
                parts += [jnp.where(lo, q2, zero), jnp.where(lo, zero, q2)]
            lhs = jnp.concatenate(parts, axis=0)
            s = lax.dot_general(lhs, kk, (((1,), (1,)), ((), ())), preferred_element_type=F32)
            s = s.reshape(GQA_GROUP, WINDOW, n_ctx + span)
            s = jnp.concatenate([
                s[:, :, :n_ctx],
                jnp.where(valid_first[j][None], s[:, :, n_ctx:n_ctx + WINDOW], MASK_VALUE),
                s[:, :, n_ctx + WINDOW:n_ctx + 2 * WINDOW],
                jnp.where(valid_last[j][None], s[:, :, n_ctx + 2 * WINDOW:], MASK_VALUE),
            ], axis=-1)
            m = jnp.maximum(jnp.max(s, axis=-1, keepdims=True), sink)
            p = jnp.exp2(s - m)
            denom = jnp.sum(p, axis=-1, keepdims=True) + jnp.exp2(sink - m)
            o = jnp.dot(p.astype(BF16).reshape(GQA_GROUP * WINDOW, n_ctx + span), vv, preferred_element_type=F32)
            o = o.reshape(GQA_GROUP, WINDOW, LANES) / denom
            for t in range(GQA_GROUP // 2):
                o2 = jnp.where(lo, o[2 * t], o[2 * t + 1])
                o_ref[0, rows, (2 * kh + t) * LANES:(2 * kh + t + 1) * LANES] = o2.astype(o_ref.dtype)


def _attention(q, kd, vd, kvc, sink, *, tq):
    b, s, qd = q.shape
    kvd = kd.shape[2]
    n_ctx = kvc.shape[1]
    nblk = s // WINDOW
    per = tq // WINDOW
    kern = functools.partial(_attn_kernel, tq=tq, seq=s, n_ctx=n_ctx)
    main = lambda bi, i: (bi, i, 0)
    prev = lambda bi, i: (bi, jnp.maximum(i * per - 1, 0), 0)
    nxt = lambda bi, i: (bi, jnp.minimum((i + 1) * per, nblk - 1), 0)
    return pl.pallas_call(
        kern,
        grid=(b, s // tq),
        in_specs=[
            pl.BlockSpec(memory_space=pltpu.SMEM),
            pl.BlockSpec((1, tq, qd), main),
            pl.BlockSpec((1, tq, kvd), main),
            pl.BlockSpec((1, WINDOW, kvd), prev),
            pl.BlockSpec((1, WINDOW, kvd), nxt),
            pl.BlockSpec((1, tq, kvd), main),
            pl.BlockSpec((1, WINDOW, kvd), prev),
            pl.BlockSpec((1, WINDOW, kvd), nxt),
            pl.BlockSpec((1, n_ctx, 2 * kvd), lambda bi, i: (bi, 0, 0)),
        ],
        out_specs=pl.BlockSpec((1, tq, qd), main),
        out_shape=jax.ShapeDtypeStruct((b, s, qd), BF16),
        compiler_params=_cparams(("parallel", "parallel")),
        name="window_attention",
    )(sink, q, kd, kd, kd, vd, vd, vd, kvc)


def _tail0_kernel(a_ref, x_ref, mod0_ref, mod1_ref, wo_ref, w1_ref, w3_ref, w2_ref, win_ref, ln_ref,
                  x_out, u_out, bg_out, *, fc):
    m0 = mod0_ref[0]
    m1 = mod1_ref[0]
    ln = ln_ref[...]
    d = x_ref.shape[1]
    mix = jnp.dot(a_ref[...], wo_ref[...], preferred_element_type=F32)
    x1 = _deepnorm_ln(x_ref[...], mix, m0[2:3, :], ln[0:1, :], ln[1:2, :])
    hb = (x1 * (1.0 + m0[4:5, :]) + m0[3:4, :]).astype(BF16)
    acc = jnp.zeros(x1.shape, F32)
    for c0 in range(0, w1_ref.shape[1], fc):
        a = jnp.dot(hb, w1_ref[:, c0:c0 + fc], preferred_element_type=F32)
        bb = jnp.dot(hb, w3_ref[:, c0:c0 + fc], preferred_element_type=F32)
        gact = (_silu(a) * bb).astype(BF16)
        acc = acc + jnp.dot(gact, w2_ref[c0:c0 + fc, :], preferred_element_type=F32)
    x2 = _deepnorm_ln(x1, acc, m0[5:6, :], ln[2:3, :], ln[3:4, :])
    x_out[...] = x2
    hb = (x2 * (1.0 + m1[1:2, :]) + m1[0:1, :]).astype(BF16)
    bg_out[...] = jnp.dot(hb, win_ref[:, :d], preferred_element_type=F32).astype(bg_out.dtype)
    cg = jnp.dot(hb, win_ref[:, d:2 * d], preferred_element_type=F32)
    val = jnp.dot(hb, win_ref[:, 2 * d:], preferred_element_type=F32)
    u_out[...] = (cg * val).astype(u_out.dtype)


def _layer0_tail(a, x, mod0, mod1, wo, w1, w3, w2, win, ln, *, rows_per_mod, tm, fc):
    t, d = x.shape
    spm = rows_per_mod // tm
    row = lambda i: (i, 0)
    modi = lambda i: (i // spm, 0, 0)
    return pl.pallas_call(
        functools.partial(_tail0_kernel, fc=fc),
        grid=(t // tm,),
        in_specs=[
            pl.BlockSpec((tm, a.shape[1]), row),
            pl.BlockSpec((tm, d), row),
            pl.BlockSpec((1, MOD_ROWS, d), modi),
            pl.BlockSpec((1, MOD_ROWS, d), modi),
            _const_spec(wo.shape),
            _const_spec(w1.shape),
            _const_spec(w3.shape),
            _const_spec(w2.shape),
            _const_spec(win.shape),
            _const_spec(ln.shape),
        ],
        out_specs=[pl.BlockSpec((tm, d), row)] * 3,
        out_shape=[jax.ShapeDtypeStruct((t, d), F32), jax.ShapeDtypeStruct((t, d), BF16),
                   jax.ShapeDtypeStruct((t, d), BF16)],
        compiler_params=_cparams(("parallel",)),
        name="oproj_ffn_convin",
    )(a, x, mod0, mod1, wo, w1, w3, w2, win, ln)


HALO = 16
RT_G1, RT_G2 = 8, 9
PLAN_ROWS = 8


def _convout_kernel(u_ref, up_ref, un_ref, bg_ref, cw_ref, w_ref, x_ref, mod_ref, g_ref, b_ref, wr_ref,
                    x_out, h_out, route_out, plan_out, cnt_out, carry_ref):
    j = pl.program_id(1)
    nj = pl.num_programs(1)

    @pl.when((pl.program_id(0) == 0) & (j == 0))
    def _():
        carry_ref[...] = jnp.zeros_like(carry_ref)

    m = mod_ref[0]
    u = u_ref[0].astype(F32)
    tm = u.shape[0]
    row = lax.broadcasted_iota(jnp.int32, (tm, 1), 0)
    prev_row = jnp.where(j == 0, 0.0, up_ref[0, HALO - 1:HALO, :].astype(F32))
    next_row = jnp.where(j == nj - 1, 0.0, un_ref[0, 0:1, :].astype(F32))
    u_prev = jnp.where(row == 0, prev_row, pltpu.roll(u, 1, 0))
    u_next = jnp.where(row == tm - 1, next_row, pltpu.roll(u, tm - 1, 0))
    cw = cw_ref[...]
    y = cw[0:1, :] * u_prev + cw[1:2, :] * u + cw[2:3, :] * u_next
    z = (bg_ref[0].astype(F32) * y).astype(BF16)
    mix = jnp.dot(z, w_ref[...], preferred_element_type=F32)
    x_new = _deepnorm_ln(x_ref[0], mix, m[2:3, :], g_ref[...], b_ref[...])
    x_out[0] = x_new
    h = x_new * (1.0 + m[4:5, :]) + m[3:4, :]
    h_out[0] = h
    hb = h.astype(BF16)
    lg = jnp.dot(hb, wr_ref[...], preferred_element_type=F32)
    lane = lax.broadcasted_iota(jnp.int32, lg.shape, 1).astype(F32)
    neg = jnp.float32(-jnp.inf)
    lg = jnp.where(lane < N_EXPERTS, lg, neg)
    m1 = jnp.max(lg, axis=-1, keepdims=True)
    i1 = jnp.min(jnp.where(lg == m1, lane, float(LANES)), axis=-1, keepdims=True)
    lg2 = jnp.where(lane == i1, neg, lg)
    m2 = jnp.max(lg2, axis=-1, keepdims=True)
    i2 = jnp.min(jnp.where(lg2 == m2, lane, float(LANES)), axis=-1, keepdims=True)
    e2 = jnp.exp(m2 - m1)
    gate1 = 1.0 / (1.0 + e2)
    gate2 = e2 / (1.0 + e2)
    member = jnp.where((lane == i1) | (lane == i2), 1.0, 0.0)
    route_out[0] = jnp.where(lane == RT_G1, gate1, jnp.where(lane == RT_G2, gate2, 0.0))
    r = lax.broadcasted_iota(jnp.int32, (tm, tm), 0)
    c = lax.broadcasted_iota(jnp.int32, (tm, tm), 1)
    tri = jnp.where(c <= r, 1.0, 0.0).astype(BF16)
    incl = jnp.dot(tri, member.astype(BF16), preferred_element_type=F32)
    before = incl - member + carry_ref[0:1, :]
    rank1 = jnp.sum(jnp.where(lane == i1, before, 0.0), axis=-1, keepdims=True)
    rank2 = jnp.sum(jnp.where(lane == i2, before, 0.0), axis=-1, keepdims=True)
    rec = jnp.where(lane == 0, rank1, jnp.where(lane == 1, rank2,
                    jnp.where(lane == 2, i1, jnp.where(lane == 3, i2, 0.0))))
    plan_out[...] = rec.T[:PLAN_ROWS, :].astype(jnp.int32)
    carry_ref[0:1, :] = carry_ref[0:1, :] + incl[tm - 1:tm, :]
    cnt_out[...] = carry_ref[...]


def _conv_out(u, bg, cw, w, x, mod, g, b, wr, *, tm):
    bsz, s, d = x.shape
    per = tm // HALO
    nh = s // HALO
    nj = s // tm
    main = lambda bi, j: (bi, j, 0)
    return pl.pallas_call(
        _convout_kernel,
        grid=(bsz, s // tm),
        in_specs=[
            pl.BlockSpec((1, tm, d), main),
            pl.BlockSpec((1, HALO, d), lambda bi, j: (bi, jnp.maximum(j * per - 1, 0), 0)),
            pl.BlockSpec((1, HALO, d), lambda bi, j: (bi, jnp.minimum((j + 1) * per, nh - 1), 0)),
            pl.BlockSpec((1, tm, d), main),
            _const_spec(cw.shape),
            _const_spec(w.shape),
            pl.BlockSpec((1, tm, d), main),
            pl.BlockSpec((1, MOD_ROWS, d), lambda bi, j: (bi, 0, 0)),
            _const_spec((1, d)),
            _const_spec((1, d)),
            _const_spec(wr.shape),
        ],
        out_specs=[
            pl.BlockSpec((1, tm, d), main),
            pl.BlockSpec((1, tm, d), main),
            pl.BlockSpec((1, tm, LANES), main),
            pl.BlockSpec((PLAN_ROWS, tm), lambda bi, j: (0, bi * nj + j)),
            pl.BlockSpec((8, LANES), lambda bi, j: (0, 0)),
        ],
        out_shape=[
            jax.ShapeDtypeStruct((bsz, s, d), F32),
            jax.ShapeDtypeStruct((bsz, s, d), F32),
            jax.ShapeDtypeStruct((bsz, s, LANES), F32),
            jax.ShapeDtypeStruct((PLAN_ROWS, bsz * s), jnp.int32),
            jax.ShapeDtypeStruct((8, LANES), F32),
        ],
        scratch_shapes=[pltpu.VMEM((8, LANES), F32)],
        compiler_params=_cparams(("arbitrary", "arbitrary")),
        name="conv_out_ln_router",
    )(u, u, u, bg, cw, w, x, mod, g, b, wr)


MOE_TILE = 512


def _row_copy(src_ref, src_row, dst_ref, dst_row, sem):
    return pltpu.make_async_copy(src_ref.at[pl.ds(src_row, 1)], dst_ref.at[pl.ds(dst_row, 1)], sem)


DMA_UNROLL = 8


def _dispatch_kernel(d1_ref, d2_ref, padlo_ref, padhi_ref, na_ref, h_ref, w1_ref, w3_ref, w2_ref,
                     xs_ref, w1_out, w3_out, w2_out, zero_ref, sem, sem2, zsem, *, tile):
    i = pl.program_id(0)
    tm = h_ref.shape[0]
    n_tiles = xs_ref.shape[0] // tile

    @pl.when(i == 0)
    def _():
        zero_ref[...] = jnp.zeros_like(zero_ref)

        def zrow_start(r, carry):
            _row_copy(zero_ref, 0, xs_ref, r, zsem).start()
            return carry

        def zrow_wait(r, carry):
            _row_copy(zero_ref, 0, xs_ref, r, zsem).wait()
            return carry

        def ztile_start(k, carry):
            pltpu.make_async_copy(zero_ref, xs_ref.at[pl.ds(k * tile, tile)], zsem).start()
            return carry

        def ztile_wait(k, carry):
            pltpu.make_async_copy(zero_ref, xs_ref.at[pl.ds(k * tile, tile)], zsem).wait()
            return carry

        for e in range(N_EXPERTS):
            lax.fori_loop(padlo_ref[e], padhi_ref[e], zrow_start, 0)
            lax.fori_loop(padlo_ref[e], padhi_ref[e], zrow_wait, 0)
        lax.fori_loop(na_ref[0], n_tiles, ztile_start, 0)
        lax.fori_loop(na_ref[0], n_tiles, ztile_wait, 0)

    base = i * tm

    def start(r, carry):
        _row_copy(h_ref, r, xs_ref, d1_ref[base + r], sem).start(priority=0)
        _row_copy(h_ref, r, xs_ref, d2_ref[base + r], sem2).start(priority=1)
        return carry

    def wait(r, carry):
        _row_copy(h_ref, r, xs_ref, d1_ref[base + r], sem).wait()
        _row_copy(h_ref, r, xs_ref, d2_ref[base + r], sem2).wait()
        return carry

    lax.fori_loop(0, tm, start, 0, unroll=DMA_UNROLL)
    w1_out[...] = w1_ref[...].astype(w1_out.dtype)
    w3_out[...] = w3_ref[...].astype(w3_out.dtype)
    w2_out[...] = w2_ref[...].astype(w2_out.dtype)
    lax.fori_loop(0, tm, wait, 0, unroll=DMA_UNROLL)


def _moe_dispatch(h, dest1, dest2, pad_lo, pad_hi, n_active, w1, w3, w2, *, n_rows, tm, tile):
    t, d = h.shape
    steps = t // tm
    ne, _, ff = w1.shape
    w13 = (ne * d // steps, ff)
    w2s = (ne * ff // steps, d)
    smem = pl.BlockSpec(memory_space=pltpu.SMEM)
    dma = pltpu.SemaphoreType.DMA(())
    rows = lambda i: (i, 0)
    xs, w1b, w3b, w2b = pl.pallas_call(
        functools.partial(_dispatch_kernel, tile=tile),
        grid=(steps,),
        in_specs=[smem, smem, smem, smem, smem, pl.BlockSpec((tm, d), rows),
                  pl.BlockSpec(w13, rows), pl.BlockSpec(w13, rows), pl.BlockSpec(w2s, rows)],
        out_specs=[pl.BlockSpec(memory_space=pl.ANY),
                   pl.BlockSpec(w13, rows), pl.BlockSpec(w13, rows), pl.BlockSpec(w2s, rows)],
        out_shape=[jax.ShapeDtypeStruct((n_rows, d), F32),
                   jax.ShapeDtypeStruct((ne * d, ff), BF16), jax.ShapeDtypeStruct((ne * d, ff), BF16),
                   jax.ShapeDtypeStruct((ne * ff, d), BF16)],
        scratch_shapes=[pltpu.VMEM((tile, d), F32), dma, dma, dma],
        compiler_params=_cparams(("arbitrary",)),
        name="moe_dispatch",
    )(dest1, dest2, pad_lo, pad_hi, n_active, h, w1.reshape(ne * d, ff), w3.reshape(ne * d, ff),
      w2.reshape(ne * ff, d))
    return xs, w1b.reshape(ne, d, ff), w3b.reshape(ne, d, ff), w2b.reshape(ne, ff, d)


def _expert_kernel(te_ref, na_ref, x_ref, w1_ref, w3_ref, w2_ref, o_ref, *, fc):
    i = pl.program_id(0)

    @pl.when(i < na_ref[0])
    def _():
        xb = x_ref[...].astype(BF16)
        acc = jnp.zeros(xb.shape, F32)
        for c0 in range(0, w1_ref.shape[2], fc):
            a = jnp.dot(xb, w1_ref[0, :, c0:c0 + fc], preferred_element_type=F32)
            bb = jnp.dot(xb, w3_ref[0, :, c0:c0 + fc], preferred_element_type=F32)
            gact = (_silu(a) * bb).astype(BF16)
            acc = acc + jnp.dot(gact, w2_ref[0, c0:c0 + fc, :], preferred_element_type=F32)
        o_ref[...] = acc

    @pl.when(i >= na_ref[0])
    def _():
        o_ref[...] = jnp.zeros_like(o_ref)


def _moe_experts(xs, tile_expert, n_active, w1, w3, w2, *, tm, fc):
    n_rows, d = xs.shape
    ff = w1.shape[2]

    def wspec(shape):
        return pl.BlockSpec(shape, lambda i, te, na: (te[i], 0, 0), pipeline_mode=pl.Buffered(1))

    grid_spec = pltpu.PrefetchScalarGridSpec(
        num_scalar_prefetch=2,
        grid=(n_rows // tm,),
        in_specs=[
            pl.BlockSpec((tm, d), lambda i, te, na: (jnp.minimum(i, na[0] - 1), 0)),
            wspec((1, d, ff)),
            wspec((1, d, ff)),
            wspec((1, ff, d)),
        ],
        out_specs=pl.BlockSpec((tm, d), lambda i, te, na: (i, 0)),
    )
    return pl.pallas_call(
        functools.partial(_expert_kernel, fc=fc),
        grid_spec=grid_spec,
        out_shape=jax.ShapeDtypeStruct((n_rows, d), F32),
        compiler_params=_cparams(("arbitrary",)),
        name="moe_experts",
    )(tile_expert, n_active, xs, w1, w3, w2)


def _combine_kernel(d1_ref, d2_ref, ys_ref, rt_ref, x_ref, mod_ref, g_ref, b_ref, o_ref, y1_ref, y2_ref, sem, sem2):
    i = pl.program_id(0)
    tm = x_ref.shape[0]
    base = i * tm

    def start(r, carry):
        _row_copy(ys_ref, d1_ref[base + r], y1_ref, r, sem).start(priority=0)
        _row_copy(ys_ref, d2_ref[base + r], y2_ref, r, sem2).start(priority=1)
        return carry

    def wait(r, carry):
        _row_copy(ys_ref, d1_ref[base + r], y1_ref, r, sem).wait()
        _row_copy(ys_ref, d2_ref[base + r], y2_ref, r, sem2).wait()
        return carry

    lax.fori_loop(0, tm, start, 0, unroll=DMA_UNROLL)
    lax.fori_loop(0, tm, wait, 0, unroll=DMA_UNROLL)
    rt = rt_ref[...]
    y = rt[:, RT_G1:RT_G1 + 1] * y1_ref[...] + rt[:, RT_G2:RT_G2 + 1] * y2_ref[...]
    o_ref[...] = _deepnorm_ln(x_ref[...], y, mod_ref[0][5:6, :], g_ref[...], b_ref[...])


def _moe_combine_ln(ys, dest1, dest2, rt, x, mod, g, b, *, rows_per_mod, tm):
    t, d = x.shape
    spm = rows_per_mod // tm
    smem = pl.BlockSpec(memory_space=pltpu.SMEM)
    dma = pltpu.SemaphoreType.DMA(())
    return pl.pallas_call(
        _combine_kernel,
        grid=(t // tm,),
        in_specs=[
            smem, smem,
            pl.BlockSpec(memory_space=pl.ANY),
            pl.BlockSpec((tm, LANES), lambda i: (i, 0)),
            pl.BlockSpec((tm, d), lambda i: (i, 0)),
            pl.BlockSpec((1, MOD_ROWS, d), lambda i: (i // spm, 0, 0)),
            _const_spec((1, d)),
            _const_spec((1, d)),
        ],
        out_specs=pl.BlockSpec((tm, d), lambda i: (i, 0)),
        out_shape=jax.ShapeDtypeStruct((t, d), F32),
        scratch_shapes=[pltpu.VMEM((tm, d), F32), pltpu.VMEM((tm, d), F32), dma, dma],
        compiler_params=_cparams(("arbitrary",)),
        name="moe_combine_ln",
    )(dest1, dest2, ys, rt, x, mod, g, b)


def _moe_layout(plan, counts, *, tile):
    n_pairs = plan.shape[1] * 2
    n_tiles = n_pairs // tile + N_EXPERTS
    cnt = counts[0, :N_EXPERTS].astype(jnp.int32)
    padded = (cnt + tile - 1) // tile * tile
    ends = jnp.cumsum(padded)
    starts = ends - padded
    dest1 = starts[plan[2]] + plan[0]
    dest2 = starts[plan[3]] + plan[1]
    n_active = ends[-1:] // tile
    tiles = jnp.minimum(jnp.arange(n_tiles, dtype=jnp.int32), n_active[0] - 1) * tile
    tile_expert = jnp.sum((ends[None, :] <= tiles[:, None]).astype(jnp.int32), axis=1)
    tile_expert = jnp.minimum(tile_expert, N_EXPERTS - 1)
    return dest1, dest2, starts + cnt, ends, tile_expert, n_active.astype(jnp.int32), n_tiles * tile


def _rope_tables(rows_count):
    half = HEAD_DIM // 2
    rows = jnp.repeat(jnp.arange(rows_count, dtype=F32), GRID_W)
    cols = jnp.tile(jnp.arange(GRID_W, dtype=F32), rows_count)
    inv_freq = ROPE_THETA ** (-jnp.arange(0, half, 2, dtype=F32) / half)
    ar = rows[:, None] * inv_freq
    ac = cols[:, None] * inv_freq
    ang = jnp.concatenate([ar, ar, ac, ac], axis=-1)
    sign = jnp.where((jnp.arange(HEAD_DIM) % 32) < 16, -1.0, 1.0).astype(F32)
    cos = jnp.tile(jnp.cos(ang), (1, LANES // HEAD_DIM))
    sin = jnp.tile(jnp.sin(ang) * sign, (1, LANES // HEAD_DIM))
    return cos, sin


def _dup_heads(w, n_heads):
    d = w.shape[0]
    w = w.reshape(d, n_heads, 1, HEAD_DIM)
    return jnp.broadcast_to(w, (d, n_heads, 2, HEAD_DIM)).reshape(d, n_heads * 2 * HEAD_DIM)


def _mod_blocks(mod_rows, d):
    r = mod_rows.shape[0]
    m = mod_rows.reshape(r, 6, d)
    return jnp.concatenate([m, jnp.zeros((r, MOD_ROWS - 6, d), F32)], axis=1)


def kernel(x, c, ctx, c_ctx, w_mod, b_mod, ln_g, ln_b, attn_w_qkv, attn_w_o, attn_sink,
           conv_w_in, conv_w, conv_w_out, ffn_w1, ffn_w3, ffn_w2,
           moe_router, moe_w1, moe_w3, moe_w2):
    bsz, seq, d = x.shape
    n_ctx = ctx.shape[1]
    t = bsz * seq
    q_dim = attn_w_o.shape[1]
    kv_dim = (attn_w_qkv.shape[2] - q_dim) // 2
    n_kv = kv_dim // HEAD_DIM

    pad_rows = (-(bsz + 1)) % 8
    cvec = jnp.concatenate([c, c_ctx[None, :], jnp.zeros((pad_rows, d), F32)], axis=0)
    mod_all = _modulation(cvec, w_mod, b_mod)
    mod0 = _mod_blocks(mod_all[0, :bsz], d)
    modc0 = _mod_blocks(mod_all[0, bsz:bsz + 1], d)
    mod1 = _mod_blocks(mod_all[1, :bsz], d)

    cos, sin = _rope_tables(seq // GRID_W)

    wqkv = attn_w_qkv[0]
    wk = _dup_heads(wqkv[:, q_dim:q_dim + kv_dim], n_kv)
    wv = _dup_heads(wqkv[:, q_dim + kv_dim:], n_kv)
    w_all = jnp.concatenate([wqkv[:, :q_dim], wk, wv], axis=1).astype(BF16)
    w_kv = jnp.concatenate([wk, wv], axis=1).astype(BF16)
    x2d = x.reshape(t, d)
    q, kd, vd = _project(x2d, mod0, w_all, cos, sin, rows_per_mod=seq, tm=512,
                         rope_lanes=q_dim + 2 * kv_dim, q_lanes=q_dim,
                         splits=(q_dim, 2 * kv_dim, 2 * kv_dim), name="qkv_rope")
    (kvc,) = _project(ctx.reshape(bsz * n_ctx, d), modc0, w_kv, cos, sin, rows_per_mod=bsz * n_ctx, tm=512,
                      rope_lanes=0, q_lanes=0, splits=(4 * kv_dim,), name="ctx_kv")
    attn = _attention(q.reshape(bsz, seq, q_dim), kd.reshape(bsz, seq, 2 * kv_dim), vd.reshape(bsz, seq, 2 * kv_dim),
                      kvc.reshape(bsz, n_ctx, 4 * kv_dim), attn_sink[0], tq=256)
    ln0 = jnp.concatenate([ln_g[0, 0][None], ln_b[0, 0][None], ln_g[0, 1][None], ln_b[0, 1][None],
                           jnp.zeros((4, d), F32)], axis=0)
    x2, u, bg = _layer0_tail(attn.reshape(t, q_dim), x2d, mod0, mod1, attn_w_o[0].astype(BF16),
                             ffn_w1[0].astype(BF16), ffn_w3[0].astype(BF16), ffn_w2[0].astype(BF16),
                             conv_w_in[0].astype(BF16), ln0, rows_per_mod=seq, tm=512, fc=256)

    cw = jnp.concatenate([conv_w[0], jnp.zeros((8 - CONV_WIDTH, d), F32)], axis=0)
    wr = jnp.concatenate([moe_router[0], jnp.zeros((d, LANES - N_EXPERTS), F32)], axis=1).astype(BF16)
    x3, h4, rt, plan, counts = _conv_out(u.reshape(bsz, seq, d), bg.reshape(bsz, seq, d), cw,
                                         conv_w_out[0].astype(BF16), x2.reshape(bsz, seq, d), mod1,
                                         ln_g[1, 0][None], ln_b[1, 0][None], wr, tm=512)
    rt = rt.reshape(t, LANES)
    dest1, dest2, pad_lo, pad_hi, tile_expert, n_active, n_rows = _moe_layout(plan, counts, tile=MOE_TILE)
    xs, w1b, w3b, w2b = _moe_dispatch(h4.reshape(t, d), dest1, dest2, pad_lo, pad_hi, n_active, moe_w1[0], moe_w3[0], moe_w2[0],
                                      n_rows=n_rows, tm=512, tile=MOE_TILE)
    ys = _moe_experts(xs, tile_expert, n_active, w1b, w3b, w2b, tm=MOE_TILE, fc=512)
    out = _moe_combine_ln(ys, dest1, dest2, rt, x3.reshape(t, d), mod1, ln_g[1, 1][None], ln_b[1, 1][None],
                          rows_per_mod=seq, tm=512)
    return out.reshape(bsz, seq, d)
```

```python
import functools

import jax
import jax.numpy as jnp
from jax import lax
from jax.experimental import pallas as pl
from jax.experimental.pallas import tpu as pltpu

F32 = jnp.float32
BF16 = jnp.bfloat16

GRID_W = 64
HEAD_DIM = 64
GQA_GROUP = 4
WINDOW = 128
ROPE_THETA = 10000.0
CONV_WIDTH = 3
N_EXPERTS = 8
LN_EPS = 1e-5
MASK_VALUE = -1e30
DEPTH = 2
DEEPNORM_ALPHA = (2 * DEPTH) ** 0.25
LOG2_E = 1.4426950408889634

LANES = 128
MOD_ROWS = 8
VMEM_LIMIT = 56 * 1024 * 1024


def _cparams(sem):
    return pltpu.CompilerParams(dimension_semantics=sem, vmem_limit_bytes=VMEM_LIMIT)


def _const_spec(shape):
    nd = len(shape)
    return pl.BlockSpec(shape, lambda *_: (0,) * nd, pipeline_mode=pl.Buffered(1))


def _silu(a):
    return a / (1.0 + jnp.exp(-a))


SUBLANES = 8


def _deepnorm_ln(x, mix, gate, g, b):
    y = DEEPNORM_ALPHA * x + gate * mix
    mu = jnp.mean(y, axis=-1, keepdims=True)
    d = y - mu
    var = jnp.mean(d * d, axis=-1, keepdims=True)
    return d * lax.rsqrt(var + LN_EPS) * g + b


def _mod_kernel(c_ref, w_ref, b_ref, o_ref):
    cv = c_ref[...]
    s = _silu(cv).astype(BF16)
    o_ref[0] = jnp.dot(s, w_ref[0].astype(BF16), preferred_element_type=F32) + b_ref[0]


def _modulation(cvec, w_mod, b_mod):
    depth, d, n = w_mod.shape
    rows = cvec.shape[0]
    nt = 1536
    return pl.pallas_call(
        _mod_kernel,
        grid=(depth, n // nt),
        in_specs=[
            pl.BlockSpec((rows, d), lambda l, j: (0, 0)),
            pl.BlockSpec((1, d, nt), lambda l, j: (l, 0, j)),
            pl.BlockSpec((1, 1, nt), lambda l, j: (l, 0, j)),
        ],
        out_specs=pl.BlockSpec((1, rows, nt), lambda l, j: (l, 0, j)),
        out_shape=jax.ShapeDtypeStruct((depth, rows, n), F32),
        compiler_params=_cparams(("parallel", "parallel")),
        name="modulation",
    )(cvec, w_mod, b_mod.reshape(depth, 1, n))


def _proj_kernel(x_ref, mod_ref, w_ref, cos_ref, sin_ref, *o_refs, rope_lanes, q_lanes, splits):
    m = mod_ref[0]
    h = x_ref[...] * (1.0 + m[1:2, :]) + m[0:1, :]
    y = jnp.dot(h.astype(BF16), w_ref[...], preferred_element_type=F32)
    if rope_lanes:
        cos = cos_ref[...]
        sin = sin_ref[...]
        first_half = (lax.broadcasted_iota(jnp.int32, (1, LANES), 1) % 32) < 16
    off = 0
    for o_ref, width in zip(o_refs, splits):
        for c0 in range(0, width, LANES):
            yc = y[:, off + c0:off + c0 + LANES]
            if off + c0 < rope_lanes:
                rot = jnp.where(first_half, pltpu.roll(yc, LANES - 16, 1), pltpu.roll(yc, 16, 1))
                yc = yc * cos + rot * sin
                if off + c0 < q_lanes:
                    yc = yc * (HEAD_DIM ** -0.5 * LOG2_E)
            o_ref[:, c0:c0 + LANES] = yc.astype(o_ref.dtype)
        off += width


def _project(x, mod, w, cos, sin, *, rows_per_mod, tm, rope_lanes, q_lanes, splits, name):
    t, d = x.shape
    n = w.shape[1]
    steps_per_mod = rows_per_mod // tm
    steps_per_seq = cos.shape[0] // tm
    kern = functools.partial(_proj_kernel, rope_lanes=rope_lanes, q_lanes=q_lanes, splits=splits)
    return pl.pallas_call(
        kern,
        grid=(t // tm,),
        in_specs=[
            pl.BlockSpec((tm, d), lambda i: (i, 0)),
            pl.BlockSpec((1, MOD_ROWS, d), lambda i: (i // steps_per_mod, 0, 0)),
            _const_spec((d, n)),
            pl.BlockSpec((tm, LANES), lambda i: (i % steps_per_seq, 0)),
            pl.BlockSpec((tm, LANES), lambda i: (i % steps_per_seq, 0)),
        ],
        out_specs=[pl.BlockSpec((tm, s), lambda i: (i, 0)) for s in splits],
        out_shape=[jax.ShapeDtypeStruct((t, s), BF16) for s in splits],
        compiler_params=_cparams(("parallel",)),
        name=name,
    )(x, mod, w, cos, sin)


def _attn_kernel(sink_ref, q_ref, km_ref, kp_ref, kn_ref, vm_ref, vp_ref, vn_ref, kvc_ref, o_ref, *, tq, seq, n_ctx):
    i = pl.program_id(1)
    nsub = tq // WINDOW
    span = 3 * WINDOW
    nkv = km_ref.shape[2] // LANES
    lane = lax.broadcasted_iota(jnp.int32, (1, LANES), 1)
    lo = lane < HEAD_DIM
    r = lax.broadcasted_iota(jnp.int32, (WINDOW, WINDOW), 0)
    c = lax.broadcasted_iota(jnp.int32, (WINDOW, WINDOW), 1)
    valid_first, valid_last = [], []
    for j in range(nsub):
        blk = i * (tq // WINDOW) + j
        valid_first.append((c >= r) & (blk > 0))
        valid_last.append((c <= r) & (blk < seq // WINDOW - 1))
    gidx = lax.broadcasted_iota(jnp.int32, (GQA_GROUP, 1, 1), 0)
    zero = jnp.zeros((), BF16)
    for kh in range(nkv):
        ks = slice(kh * LANES, (kh + 1) * LANES)
        kall = jnp.concatenate([kp_ref[0, :, ks], km_ref[0, :, ks], kn_ref[0, :, ks]], axis=0)
        vall = jnp.concatenate([vp_ref[0, :, ks], vm_ref[0, :, ks], vn_ref[0, :, ks]], axis=0)
        kc = kvc_ref[0, :, ks]
        vc = kvc_ref[0, :, nkv * LANES + kh * LANES:nkv * LANES + (kh + 1) * LANES]
        sink = jnp.full((GQA_GROUP, 1, 1), sink_ref[kh * GQA_GROUP], F32)
        for g in range(1, GQA_GROUP):
            sink = jnp.where(gidx == g, sink_ref[kh * GQA_GROUP + g], sink)
        sink = sink * LOG2_E
        for j in range(nsub):
            rows = slice(j * WINDOW, (j + 1) * WINDOW)
            kk = jnp.concatenate([kc, kall[j * WINDOW:j * WINDOW + span]], axis=0)
            vv = jnp.concatenate([vc, vall[j * WINDOW:j * WINDOW + span]], axis=0)
            parts = []
            for t in range(GQA_GROUP // 2):
                q2 = q_ref[0, rows, (2 * kh + t) * LANES:(2 * kh + t + 1) * LANES]
                parts += [jnp.where(lo, q2, zero), jnp.where(lo, zero, q2)]
            lhs = jnp.concatenate(parts, axis=0)
            s = lax.dot_general(lhs, kk, (((1,), (1,)), ((), ())), preferred_element_type=F32)
            s = s.reshape(GQA_GROUP, WINDOW, n_ctx + span)
            s = jnp.concatenate([
                s[:, :, :n_ctx],
                jnp.where(valid_first[j][None], s[:, :, n_ctx:n_ctx + WINDOW], MASK_VALUE),
                s[:, :, n_ctx + WINDOW:n_ctx + 2 * WINDOW],
                jnp.where(valid_last[j][None], s[:, :, n_ctx + 2 * WINDOW:], MASK_VALUE),
            ], axis=-1)
            m = jnp.maximum(jnp.max(s, axis=-1, keepdims=True), sink)
            p = jnp.exp2(s - m)
            denom = jnp.sum(p, axis=-1, keepdims=True) + jnp.exp2(sink - m)
            o = jnp.dot(p.astype(BF16).reshape(GQA_GROUP * WINDOW, n_ctx + span), vv, preferred_element_type=F32)
            o = o.reshape(GQA_GROUP, WINDOW, LANES) / denom
            for t in range(GQA_GROUP // 2):
                o2 = jnp.where(lo, o[2 * t], o[2 * t + 1])
                o_ref[0, rows, (2 * kh + t) * LANES:(2 * kh + t + 1) * LANES] = o2.astype(o_ref.dtype)


def _attention(q, kd, vd, kvc, sink, *, tq):
    b, s, qd = q.shape
    kvd = kd.shape[2]
    n_ctx = kvc.shape[1]
    nblk = s // WINDOW
    per = tq // WINDOW
    kern = functools.partial(_attn_kernel, tq=tq, seq=s, n_ctx=n_ctx)
    main = lambda bi, i: (bi, i, 0)
    prev = lambda bi, i: (bi, jnp.maximum(i * per - 1, 0), 0)
    nxt = lambda bi, i: (bi, jnp.minimum((i + 1) * per, nblk - 1), 0)
    return pl.pallas_call(
        kern,
        grid=(b, s // tq),
        in_specs=[
            pl.BlockSpec(memory_space=pltpu.SMEM),
            pl.BlockSpec((1, tq, qd), main),
            pl.BlockSpec((1, tq, kvd), main),
            pl.BlockSpec((1, WINDOW, kvd), prev),
            pl.BlockSpec((1, WINDOW, kvd), nxt),
            pl.BlockSpec((1, tq, kvd), main),
            pl.BlockSpec((1, WINDOW, kvd), prev),
            pl.BlockSpec((1, WINDOW, kvd), nxt),
            pl.BlockSpec((1, n_ctx, 2 * kvd), lambda bi, i: (bi, 0, 0)),
        ],
        out_specs=pl.BlockSpec((1, tq, qd), main),
        out_shape=jax.ShapeDtypeStruct((b, s, qd), BF16),
        compiler_params=_cparams(("parallel", "parallel")),
        name="window_attention",
    )(sink, q, kd, kd, kd, vd, vd, vd, kvc)


def _tail0_kernel(a_ref, x_ref, mod0_ref, mod1_ref, wo_ref, w1_ref, w3_ref, w2_ref, win_ref, ln_ref,
                  x_out, u_out, bg_out, *, fc):
    m0 = mod0_ref[0]
    m1 = mod1_ref[0]
    ln = ln_ref[...]
    d = x_ref.shape[1]
    mix = jnp.dot(a_ref[...], wo_ref[...], preferred_element_type=F32)
    x1 = _deepnorm_ln(x_ref[...], mix, m0[2:3, :], ln[0:1, :], ln[1:2, :])
    hb = (x1 * (1.0 + m0[4:5, :]) + m0[3:4, :]).astype(BF16)
    acc = jnp.zeros(x1.shape, F32)
    for c0 in range(0, w1_ref.shape[1], fc):
        a = jnp.dot(hb, w1_ref[:, c0:c0 + fc], preferred_element_type=F32)
        bb = jnp.dot(hb, w3_ref[:, c0:c0 + fc], preferred_element_type=F32)
        gact = (_silu(a) * bb).astype(BF16)
        acc = acc + jnp.dot(gact, w2_ref[c0:c0 + fc, :], preferred_element_type=F32)
    x2 = _deepnorm_ln(x1, acc, m0[5:6, :], ln[2:3, :], ln[3:4, :])
    x_out[...] = x2
    hb = (x2 * (1.0 + m1[1:2, :]) + m1[0:1, :]).astype(BF16)
    bg_out[...] = jnp.dot(hb, win_ref[:, :d], preferred_element_type=F32).astype(bg_out.dtype)
    cg = jnp.dot(hb, win_ref[:, d:2 * d], preferred_element_type=F32)
    val = jnp.dot(hb, win_ref[:, 2 * d:], preferred_element_type=F32)
    u_out[...] = (cg * val).astype(u_out.dtype)


def _layer0_tail(a, x, mod0, mod1, wo, w1, w3, w2, win, ln, *, rows_per_mod, tm, fc):
    t, d = x.shape
    spm = rows_per_mod // tm
    row = lambda i: (i, 0)
    modi = lambda i: (i // spm, 0, 0)
    return pl.pallas_call(
        functools.partial(_tail0_kernel, fc=fc),
        grid=(t // tm,),
        in_specs=[
            pl.BlockSpec((tm, a.shape[1]), row),
            pl.BlockSpec((tm, d), row),
            pl.BlockSpec((1, MOD_ROWS, d), modi),
            pl.BlockSpec((1, MOD_ROWS, d), modi),
            _const_spec(wo.shape),
            _const_spec(w1.shape),
            _const_spec(w3.shape),
            _const_spec(w2.shape),
            _const_spec(win.shape),
            _const_spec(ln.shape),
        ],
        out_specs=[pl.BlockSpec((tm, d), row)] * 3,
        out_shape=[jax.ShapeDtypeStruct((t, d), F32), jax.ShapeDtypeStruct((t, d), BF16),
                   jax.ShapeDtypeStruct((t, d), BF16)],
        compiler_params=_cparams(("parallel",)),
        name="oproj_ffn_convin",
    )(a, x, mod0, mod1, wo, w1, w3, w2, win, ln)


HALO = 16
RT_G1, RT_G2 = 8, 9
PLAN_ROWS = 8


def _convout_kernel(u_ref, up_ref, un_ref, bg_ref, cw_ref, w_ref, x_ref, mod_ref, g_ref, b_ref, wr_ref,
                    x_out, h_out, route_out, plan_out, cnt_out, carry_ref):
    j = pl.program_id(1)
    nj = pl.num_programs(1)

    @pl.when((pl.program_id(0) == 0) & (j == 0))
    def _():
        carry_ref[...] = jnp.zeros_like(carry_ref)

    m = mod_ref[0]
    u = u_ref[0].astype(F32)
    tm = u.shape[0]
    row = lax.broadcasted_iota(jnp.int32, (tm, 1), 0)
    prev_row = jnp.where(j == 0, 0.0, up_ref[0, HALO - 1:HALO, :].astype(F32))
    next_row = jnp.where(j == nj - 1, 0.0, un_ref[0, 0:1, :].astype(F32))
    u_prev = jnp.where(row == 0, prev_row, pltpu.roll(u, 1, 0))
    u_next = jnp.where(row == tm - 1, next_row, pltpu.roll(u, tm - 1, 0))
    cw = cw_ref[...]
    y = cw[0:1, :] * u_prev + cw[1:2, :] * u + cw[2:3, :] * u_next
    z = (bg_ref[0].astype(F32) * y).astype(BF16)
    mix = jnp.dot(z, w_ref[...], preferred_element_type=F32)
    x_new = _deepnorm_ln(x_ref[0], mix, m[2:3, :], g_ref[...], b_ref[...])
    x_out[0] = x_new
    h = x_new * (1.0 + m[4:5, :]) + m[3:4, :]
    for c in range(SUBLANES):
        h_out[0, c] = h[:, c * LANES:(c + 1) * LANES]
    hb = h.astype(BF16)
    lg = jnp.dot(hb, wr_ref[...], preferred_element_type=F32)
    lane = lax.broadcasted_iota(jnp.int32, lg.shape, 1).astype(F32)
    neg = jnp.float32(-jnp.inf)
    lg = jnp.where(lane < N_EXPERTS, lg, neg)
    m1 = jnp.max(lg, axis=-1, keepdims=True)
    i1 = jnp.min(jnp.where(lg == m1, lane, float(LANES)), axis=-1, keepdims=True)
    lg2 = jnp.where(lane == i1, neg, lg)
    m2 = jnp.max(lg2, axis=-1, keepdims=True)
    i2 = jnp.min(jnp.where(lg2 == m2, lane, float(LANES)), axis=-1, keepdims=True)
    e2 = jnp.exp(m2 - m1)
    gate1 = 1.0 / (1.0 + e2)
    gate2 = e2 / (1.0 + e2)
    member = jnp.where((lane == i1) | (lane == i2), 1.0, 0.0)
    route_out[0] = jnp.where(lane == RT_G1, gate1, jnp.where(lane == RT_G2, gate2, 0.0))
    r = lax.broadcasted_iota(jnp.int32, (tm, tm), 0)
    c = lax.broadcasted_iota(jnp.int32, (tm, tm), 1)
    tri = jnp.where(c <= r, 1.0, 0.0).astype(BF16)
    incl = jnp.dot(tri, member.astype(BF16), preferred_element_type=F32)
    before = incl - member + carry_ref[0:1, :]
    rank1 = jnp.sum(jnp.where(lane == i1, before, 0.0), axis=-1, keepdims=True)
    rank2 = jnp.sum(jnp.where(lane == i2, before, 0.0), axis=-1, keepdims=True)
    rec = jnp.where(lane == 0, rank1, jnp.where(lane == 1, rank2,
                    jnp.where(lane == 2, i1, jnp.where(lane == 3, i2, 0.0))))
    plan_out[...] = rec.T[:PLAN_ROWS, :].astype(jnp.int32)
    carry_ref[0:1, :] = carry_ref[0:1, :] + incl[tm - 1:tm, :]
    cnt_out[...] = carry_ref[...]


def _conv_out(u, bg, cw, w, x, mod, g, b, wr, *, tm):
    bsz, s, d = x.shape
    per = tm // HALO
    nh = s // HALO
    nj = s // tm
    main = lambda bi, j: (bi, j, 0)
    return pl.pallas_call(
        _convout_kernel,
        grid=(bsz, s // tm),
        in_specs=[
            pl.BlockSpec((1, tm, d), main),
            pl.BlockSpec((1, HALO, d), lambda bi, j: (bi, jnp.maximum(j * per - 1, 0), 0)),
            pl.BlockSpec((1, HALO, d), lambda bi, j: (bi, jnp.minimum((j + 1) * per, nh - 1), 0)),
            pl.BlockSpec((1, tm, d), main),
            _const_spec(cw.shape),
            _const_spec(w.shape),
            pl.BlockSpec((1, tm, d), main),
            pl.BlockSpec((1, MOD_ROWS, d), lambda bi, j: (bi, 0, 0)),
            _const_spec((1, d)),
            _const_spec((1, d)),
            _const_spec(wr.shape),
        ],
        out_specs=[
            pl.BlockSpec((1, tm, d), main),
            pl.BlockSpec((1, SUBLANES, tm, d // SUBLANES), lambda bi, j: (bi, 0, j, 0)),
            pl.BlockSpec((1, tm, LANES), main),
            pl.BlockSpec((PLAN_ROWS, tm), lambda bi, j: (0, bi * nj + j)),
            pl.BlockSpec((8, LANES), lambda bi, j: (0, 0)),
        ],
        out_shape=[
            jax.ShapeDtypeStruct((bsz, s, d), F32),
            jax.ShapeDtypeStruct((bsz, SUBLANES, s, d // SUBLANES), F32),
            jax.ShapeDtypeStruct((bsz, s, LANES), F32),
            jax.ShapeDtypeStruct((PLAN_ROWS, bsz * s), jnp.int32),
            jax.ShapeDtypeStruct((8, LANES), F32),
        ],
        scratch_shapes=[pltpu.VMEM((8, LANES), F32)],
        compiler_params=_cparams(("arbitrary", "arbitrary")),
        name="conv_out_ln_router",
    )(u, u, u, bg, cw, w, x, mod, g, b, wr)


MOE_TILE = 512


def _row_copy(src_ref, src_row, dst_ref, dst_row, sem):
    return pltpu.make_async_copy(src_ref.at[src_row], dst_ref.at[dst_row], sem)


def _chunked_row(ref, row):
    return ref.at[:, row, :]


def _chunk_copies(chunked_ref, tiled_ref, row0, sem, *, to_tiled):
    rows = chunked_ref.shape[1]
    out = []
    for c in range(SUBLANES):
        hbm = tiled_ref.at[pl.ds(row0, rows), c, :]
        vmem = chunked_ref.at[c]
        out.append(pltpu.make_async_copy(vmem, hbm, sem) if to_tiled else pltpu.make_async_copy(hbm, vmem, sem))
    return out


DMA_UNROLL = 8


def _dispatch_kernel(d1_ref, d2_ref, padlo_ref, padhi_ref, na_ref, h_ref, w1_ref, w3_ref, w2_ref,
                     xs_ref, w1_out, w3_out, w2_out, zero_ref, sem, sem2, zsem, *, tile):
    i = pl.program_id(0)
    tm = h_ref.shape[2]
    hsrc = h_ref.at[0]
    n_tiles = xs_ref.shape[0] // tile

    @pl.when(i == 0)
    def _():
        zero_ref[...] = jnp.zeros_like(zero_ref)

        def zrow_start(r, carry):
            _row_copy(zero_ref, 0, xs_ref, r, zsem).start()
            return carry

        def zrow_wait(r, carry):
            _row_copy(zero_ref, 0, xs_ref, r, zsem).wait()
            return carry

        def ztile_start(k, carry):
            pltpu.make_async_copy(zero_ref, xs_ref.at[pl.ds(k * tile, tile)], zsem).start()
            return carry

        def ztile_wait(k, carry):
            pltpu.make_async_copy(zero_ref, xs_ref.at[pl.ds(k * tile, tile)], zsem).wait()
            return carry

        for e in range(N_EXPERTS):
            lax.fori_loop(padlo_ref[e], padhi_ref[e], zrow_start, 0)
            lax.fori_loop(padlo_ref[e], padhi_ref[e], zrow_wait, 0)
        lax.fori_loop(na_ref[0], n_tiles, ztile_start, 0)
        lax.fori_loop(na_ref[0], n_tiles, ztile_wait, 0)

    base = i * tm

    def start(r, carry):
        pltpu.make_async_copy(_chunked_row(hsrc, r), xs_ref.at[d1_ref[base + r]], sem).start(priority=0)
        pltpu.make_async_copy(_chunked_row(hsrc, r), xs_ref.at[d2_ref[base + r]], sem2).start(priority=1)
        return carry

    def wait(r, carry):
        pltpu.make_async_copy(_chunked_row(hsrc, r), xs_ref.at[d1_ref[base + r]], sem).wait()
        pltpu.make_async_copy(_chunked_row(hsrc, r), xs_ref.at[d2_ref[base + r]], sem2).wait()
        return carry

    lax.fori_loop(0, tm, start, 0, unroll=DMA_UNROLL)
    w1_out[...] = w1_ref[...].astype(w1_out.dtype)
    w3_out[...] = w3_ref[...].astype(w3_out.dtype)
    w2_out[...] = w2_ref[...].astype(w2_out.dtype)
    lax.fori_loop(0, tm, wait, 0, unroll=DMA_UNROLL)


def _moe_dispatch(h, dest1, dest2, pad_lo, pad_hi, n_active, w1, w3, w2, *, n_rows, tm, tile):
    bsz, nch, seq, lanes = h.shape
    per = seq // tm
    steps = bsz * per
    ne, d, ff = w1.shape
    w13 = (ne * d // steps, ff)
    w2s = (ne * ff // steps, d)
    smem = pl.BlockSpec(memory_space=pltpu.SMEM)
    dma = pltpu.SemaphoreType.DMA(())
    rows = lambda i: (i, 0)
    xs, w1b, w3b, w2b = pl.pallas_call(
        functools.partial(_dispatch_kernel, tile=tile),
        grid=(steps,),
        in_specs=[smem, smem, smem, smem, smem,
                  pl.BlockSpec((1, nch, tm, lanes), lambda i: (i // per, 0, i % per, 0)),
                  pl.BlockSpec(w13, rows), pl.BlockSpec(w13, rows), pl.BlockSpec(w2s, rows)],
        out_specs=[pl.BlockSpec(memory_space=pl.ANY),
                   pl.BlockSpec(w13, rows), pl.BlockSpec(w13, rows), pl.BlockSpec(w2s, rows)],
        out_shape=[jax.ShapeDtypeStruct((n_rows, nch, lanes), F32),
                   jax.ShapeDtypeStruct((ne * d, ff), BF16), jax.ShapeDtypeStruct((ne * d, ff), BF16),
                   jax.ShapeDtypeStruct((ne * ff, d), BF16)],
        scratch_shapes=[pltpu.VMEM((tile, nch, lanes), F32), dma, dma, dma],
        compiler_params=_cparams(("arbitrary",)),
        name="moe_dispatch",
    )(dest1, dest2, pad_lo, pad_hi, n_active, h, w1.reshape(ne * d, ff), w3.reshape(ne * d, ff),
      w2.reshape(ne * ff, d))
    return xs, w1b.reshape(ne, d, ff), w3b.reshape(ne, d, ff), w2b.reshape(ne, ff, d)


def _expert_kernel(te_ref, na_ref, xs_ref, w1_ref, w3_ref, w2_ref, ys_ref, xbuf, obuf, xsem, osem, *, fc):
    i = pl.program_id(0)
    n = pl.num_programs(0)
    na = na_ref[0]
    tm = xbuf.shape[2]
    slot = i % 2

    def fetch(k):
        return _chunk_copies(xbuf.at[k % 2], xs_ref, k * tm, xsem.at[k % 2], to_tiled=False)

    def writeback(k):
        return _chunk_copies(obuf.at[k % 2], ys_ref, k * tm, osem.at[k % 2], to_tiled=True)

    @pl.when(i == 0)
    def _():
        for cp in fetch(i):
            cp.start()

    @pl.when(i + 1 < na)
    def _():
        for cp in fetch(i + 1):
            cp.start()

    @pl.when(i >= 2)
    def _():
        for cp in writeback(i - 2):
            cp.wait()

    @pl.when(i < na)
    def _():
        for cp in fetch(i):
            cp.wait()
        xb = jnp.concatenate([xbuf[slot, c] for c in range(SUBLANES)], axis=-1).astype(BF16)
        acc = jnp.zeros(xb.shape, F32)
        for c0 in range(0, w1_ref.shape[2], fc):
            a = jnp.dot(xb, w1_ref[0, :, c0:c0 + fc], preferred_element_type=F32)
            bb = jnp.dot(xb, w3_ref[0, :, c0:c0 + fc], preferred_element_type=F32)
            gact = (_silu(a) * bb).astype(BF16)
            acc = acc + jnp.dot(gact, w2_ref[0, c0:c0 + fc, :], preferred_element_type=F32)
        for c in range(SUBLANES):
            obuf[slot, c] = acc[:, c * LANES:(c + 1) * LANES]

    @pl.when(i >= na)
    def _():
        obuf[slot] = jnp.zeros(obuf.shape[1:], obuf.dtype)

    for cp in writeback(i):
        cp.start()

    @pl.when(i == n - 1)
    def _():
        for cp in writeback(i):
            cp.wait()

        @pl.when(i >= 1)
        def _():
            for cp in writeback(i - 1):
                cp.wait()


def _moe_experts(xs, tile_expert, n_active, w1, w3, w2, *, tm, fc):
    n_rows, nch, lanes = xs.shape
    _, d, ff = w1.shape

    def wspec(shape):
        return pl.BlockSpec(shape, lambda i, te, na: (te[i], 0, 0), pipeline_mode=pl.Buffered(1))

    grid_spec = pltpu.PrefetchScalarGridSpec(
        num_scalar_prefetch=2,
        grid=(n_rows // tm,),
        in_specs=[pl.BlockSpec(memory_space=pl.ANY), wspec((1, d, ff)), wspec((1, d, ff)), wspec((1, ff, d))],
        out_specs=pl.BlockSpec(memory_space=pl.ANY),
        scratch_shapes=[pltpu.VMEM((2, nch, tm, lanes), F32), pltpu.VMEM((2, nch, tm, lanes), F32),
                        pltpu.SemaphoreType.DMA((2,)), pltpu.SemaphoreType.DMA((2,))],
    )
    return pl.pallas_call(
        functools.partial(_expert_kernel, fc=fc),
        grid_spec=grid_spec,
        out_shape=jax.ShapeDtypeStruct((n_rows, nch, lanes), F32),
        compiler_params=_cparams(("arbitrary",)),
        name="moe_experts",
    )(tile_expert, n_active, xs, w1, w3, w2)


def _combine_kernel(d1_ref, d2_ref, ys_ref, rt_ref, x_ref, mod_ref, g_ref, b_ref, o_ref, y1_ref, y2_ref, sem, sem2):
    i = pl.program_id(0)
    tm = x_ref.shape[0]
    base = i * tm

    def start(r, carry):
        pltpu.make_async_copy(ys_ref.at[d1_ref[base + r]], _chunked_row(y1_ref, r), sem).start(priority=0)
        pltpu.make_async_copy(ys_ref.at[d2_ref[base + r]], _chunked_row(y2_ref, r), sem2).start(priority=1)
        return carry

    def wait(r, carry):
        pltpu.make_async_copy(ys_ref.at[d1_ref[base + r]], _chunked_row(y1_ref, r), sem).wait()
        pltpu.make_async_copy(ys_ref.at[d2_ref[base + r]], _chunked_row(y2_ref, r), sem2).wait()
        return carry

    lax.fori_loop(0, tm, start, 0, unroll=DMA_UNROLL)
    lax.fori_loop(0, tm, wait, 0, unroll=DMA_UNROLL)
    rt = rt_ref[...]
    y1 = jnp.concatenate([y1_ref[c] for c in range(SUBLANES)], axis=-1)
    y2 = jnp.concatenate([y2_ref[c] for c in range(SUBLANES)], axis=-1)
    y = rt[:, RT_G1:RT_G1 + 1] * y1 + rt[:, RT_G2:RT_G2 + 1] * y2
    o_ref[...] = _deepnorm_ln(x_ref[...], y, mod_ref[0][5:6, :], g_ref[...], b_ref[...])


def _moe_combine_ln(ys, dest1, dest2, rt, x, mod, g, b, *, rows_per_mod, tm):
    t, d = x.shape
    spm = rows_per_mod // tm
    smem = pl.BlockSpec(memory_space=pltpu.SMEM)
    dma = pltpu.SemaphoreType.DMA(())
    return pl.pallas_call(
        _combine_kernel,
        grid=(t // tm,),
        in_specs=[
            smem, smem,
            pl.BlockSpec(memory_space=pl.ANY),
            pl.BlockSpec((tm, LANES), lambda i: (i, 0)),
            pl.BlockSpec((tm, d), lambda i: (i, 0)),
            pl.BlockSpec((1, MOD_ROWS, d), lambda i: (i // spm, 0, 0)),
            _const_spec((1, d)),
            _const_spec((1, d)),
        ],
        out_specs=pl.BlockSpec((tm, d), lambda i: (i, 0)),
        out_shape=jax.ShapeDtypeStruct((t, d), F32),
        scratch_shapes=[pltpu.VMEM((SUBLANES, tm, LANES), F32), pltpu.VMEM((SUBLANES, tm, LANES), F32), dma, dma],
        compiler_params=_cparams(("arbitrary",)),
        name="moe_combine_ln",
    )(dest1, dest2, ys, rt, x, mod, g, b)


def _moe_layout(plan, counts, *, tile):
    n_pairs = plan.shape[1] * 2
    n_tiles = n_pairs // tile + N_EXPERTS
    cnt = counts[0, :N_EXPERTS].astype(jnp.int32)
    padded = (cnt + tile - 1) // tile * tile
    ends = jnp.cumsum(padded)
    starts = ends - padded
    dest1 = starts[plan[2]] + plan[0]
    dest2 = starts[plan[3]] + plan[1]
    n_active = ends[-1:] // tile
    tiles = jnp.minimum(jnp.arange(n_tiles, dtype=jnp.int32), n_active[0] - 1) * tile
    tile_expert = jnp.sum((ends[None, :] <= tiles[:, None]).astype(jnp.int32), axis=1)
    tile_expert = jnp.minimum(tile_expert, N_EXPERTS - 1)
    return dest1, dest2, starts + cnt, ends, tile_expert, n_active.astype(jnp.int32), n_tiles * tile


def _rope_tables(rows_count):
    half = HEAD_DIM // 2
    rows = jnp.repeat(jnp.arange(rows_count, dtype=F32), GRID_W)
    cols = jnp.tile(jnp.arange(GRID_W, dtype=F32), rows_count)
    inv_freq = ROPE_THETA ** (-jnp.arange(0, half, 2, dtype=F32) / half)
    ar = rows[:, None] * inv_freq
    ac = cols[:, None] * inv_freq
    ang = jnp.concatenate([ar, ar, ac, ac], axis=-1)
    sign = jnp.where((jnp.arange(HEAD_DIM) % 32) < 16, -1.0, 1.0).astype(F32)
    cos = jnp.tile(jnp.cos(ang), (1, LANES // HEAD_DIM))
    sin = jnp.tile(jnp.sin(ang) * sign, (1, LANES // HEAD_DIM))
    return cos, sin


def _dup_heads(w, n_heads):
    d = w.shape[0]
    w = w.reshape(d, n_heads, 1, HEAD_DIM)
    return jnp.broadcast_to(w, (d, n_heads, 2, HEAD_DIM)).reshape(d, n_heads * 2 * HEAD_DIM)


def _mod_blocks(mod_rows, d):
    r = mod_rows.shape[0]
    m = mod_rows.reshape(r, 6, d)
    return jnp.concatenate([m, jnp.zeros((r, MOD_ROWS - 6, d), F32)], axis=1)


def kernel(x, c, ctx, c_ctx, w_mod, b_mod, ln_g, ln_b, attn_w_qkv, attn_w_o, attn_sink,
           conv_w_in, conv_w, conv_w_out, ffn_w1, ffn_w3, ffn_w2,
           moe_router, moe_w1, moe_w3, moe_w2):
    bsz, seq, d = x.shape
    n_ctx = ctx.shape[1]
    t = bsz * seq
    q_dim = attn_w_o.shape[1]
    kv_dim = (attn_w_qkv.shape[2] - q_dim) // 2
    n_kv = kv_dim // HEAD_DIM

    pad_rows = (-(bsz + 1)) % 8
    cvec = jnp.concatenate([c, c_ctx[None, :], jnp.zeros((pad_rows, d), F32)], axis=0)
    mod_all = _modulation(cvec, w_mod, b_mod)
    mod0 = _mod_blocks(mod_all[0, :bsz], d)
    modc0 = _mod_blocks(mod_all[0, bsz:bsz + 1], d)
    mod1 = _mod_blocks(mod_all[1, :bsz], d)

    cos, sin = _rope_tables(seq // GRID_W)

    wqkv = attn_w_qkv[0]
    wk = _dup_heads(wqkv[:, q_dim:q_dim + kv_dim], n_kv)
    wv = _dup_heads(wqkv[:, q_dim + kv_dim:], n_kv)
    w_all = jnp.concatenate([wqkv[:, :q_dim], wk, wv], axis=1).astype(BF16)
    w_kv = jnp.concatenate([wk, wv], axis=1).astype(BF16)
    x2d = x.reshape(t, d)
    q, kd, vd = _project(x2d, mod0, w_all, cos, sin, rows_per_mod=seq, tm=512,
                         rope_lanes=q_dim + 2 * kv_dim, q_lanes=q_dim,
                         splits=(q_dim, 2 * kv_dim, 2 * kv_dim), name="qkv_rope")
    (kvc,) = _project(ctx.reshape(bsz * n_ctx, d), modc0, w_kv, cos, sin, rows_per_mod=bsz * n_ctx, tm=512,
                      rope_lanes=0, q_lanes=0, splits=(4 * kv_dim,), name="ctx_kv")
    attn = _attention(q.reshape(bsz, seq, q_dim), kd.reshape(bsz, seq, 2 * kv_dim), vd.reshape(bsz, seq, 2 * kv_dim),
                      kvc.reshape(bsz, n_ctx, 4 * kv_dim), attn_sink[0], tq=256)
    ln0 = jnp.concatenate([ln_g[0, 0][None], ln_b[0, 0][None], ln_g[0, 1][None], ln_b[0, 1][None],
                           jnp.zeros((4, d), F32)], axis=0)
    x2, u, bg = _layer0_tail(attn.reshape(t, q_dim), x2d, mod0, mod1, attn_w_o[0].astype(BF16),
                             ffn_w1[0].astype(BF16), ffn_w3[0].astype(BF16), ffn_w2[0].astype(BF16),
                             conv_w_in[0].astype(BF16), ln0, rows_per_mod=seq, tm=512, fc=256)

    cw = jnp.concatenate([conv_w[0], jnp.zeros((8 - CONV_WIDTH, d), F32)], axis=0)
    wr = jnp.concatenate([moe_router[0], jnp.zeros((d, LANES - N_EXPERTS), F32)], axis=1).astype(BF16)
    x3, h4, rt, plan, counts = _conv_out(u.reshape(bsz, seq, d), bg.reshape(bsz, seq, d), cw,
                                         conv_w_out[0].astype(BF16), x2.reshape(bsz, seq, d), mod1,
                                         ln_g[1, 0][None], ln_b[1, 0][None], wr, tm=512)
    rt = rt.reshape(t, LANES)
    dest1, dest2, pad_lo, pad_hi, tile_expert, n_active, n_rows = _moe_layout(plan, counts, tile=MOE_TILE)
    xs, w1b, w3b, w2b = _moe_dispatch(h4, dest1, dest2, pad_lo, pad_hi, n_active, moe_w1[0], moe_w3[0], moe_w2[0],
                                      n_rows=n_rows, tm=512, tile=MOE_TILE)
    ys = _moe_experts(xs, tile_expert, n_active, w1b, w3b, w2b, tm=MOE_TILE, fc=512)
    out = _moe_combine_ln(ys, dest1, dest2, rt, x3.reshape(t, d), mod1, ln_g[1, 1][None], ln_b[1, 1][None],
                          rows_per_mod=seq, tm=512)
    return out.reshape(bsz, seq, d)
```

```python
import functools

import jax
import jax.numpy as jnp
from jax import lax
from jax.experimental import pallas as pl
from jax.experimental.pallas import tpu as pltpu

F32 = jnp.float32
BF16 = jnp.bfloat16

GRID_W = 64
HEAD_DIM = 64
GQA_GROUP = 4
WINDOW = 128
ROPE_THETA = 10000.0
CONV_WIDTH = 3
N_EXPERTS = 8
LN_EPS = 1e-5
MASK_VALUE = -1e30
DEPTH = 2
DEEPNORM_ALPHA = (2 * DEPTH) ** 0.25
LOG2_E = 1.4426950408889634

LANES = 128
MOD_ROWS = 8
VMEM_LIMIT = 56 * 1024 * 1024


def _cparams(sem):
    return pltpu.CompilerParams(dimension_semantics=sem, vmem_limit_bytes=VMEM_LIMIT)


def _const_spec(shape):
    nd = len(shape)
    return pl.BlockSpec(shape, lambda *_: (0,) * nd, pipeline_mode=pl.Buffered(1))


def _silu(a):
    return a / (1.0 + jnp.exp(-a))


SUBLANES = 8


def _deepnorm_ln(x, mix, gate, g, b):
    y = DEEPNORM_ALPHA * x + gate * mix
    mu = jnp.mean(y, axis=-1, keepdims=True)
    d = y - mu
    var = jnp.mean(d * d, axis=-1, keepdims=True)
    return d * lax.rsqrt(var + LN_EPS) * g + b


def _mod_kernel(c_ref, w_ref, b_ref, o_ref):
    cv = c_ref[...]
    s = _silu(cv).astype(BF16)
    o_ref[0] = jnp.dot(s, w_ref[0].astype(BF16), preferred_element_type=F32) + b_ref[0]


def _modulation(cvec, w_mod, b_mod):
    depth, d, n = w_mod.shape
    rows = cvec.shape[0]
    nt = 1536
    return pl.pallas_call(
        _mod_kernel,
        grid=(depth, n // nt),
        in_specs=[
            pl.BlockSpec((rows, d), lambda l, j: (0, 0)),
            pl.BlockSpec((1, d, nt), lambda l, j: (l, 0, j)),
            pl.BlockSpec((1, 1, nt), lambda l, j: (l, 0, j)),
        ],
        out_specs=pl.BlockSpec((1, rows, nt), lambda l, j: (l, 0, j)),
        out_shape=jax.ShapeDtypeStruct((depth, rows, n), F32),
        compiler_params=_cparams(("parallel", "parallel")),
        name="modulation",
    )(cvec, w_mod, b_mod.reshape(depth, 1, n))


def _proj_kernel(x_ref, mod_ref, w_ref, cos_ref, sin_ref, *o_refs, rope_lanes, q_lanes, splits):
    m = mod_ref[0]
    h = x_ref[...] * (1.0 + m[1:2, :]) + m[0:1, :]
    y = jnp.dot(h.astype(BF16), w_ref[...], preferred_element_type=F32)
    if rope_lanes:
        cos = cos_ref[...]
        sin = sin_ref[...]
        first_half = (lax.broadcasted_iota(jnp.int32, (1, LANES), 1) % 32) < 16
    off = 0
    for o_ref, width in zip(o_refs, splits):
        for c0 in range(0, width, LANES):
            yc = y[:, off + c0:off + c0 + LANES]
            if off + c0 < rope_lanes:
                rot = jnp.where(first_half, pltpu.roll(yc, LANES - 16, 1), pltpu.roll(yc, 16, 1))
                yc = yc * cos + rot * sin
                if off + c0 < q_lanes:
                    yc = yc * (HEAD_DIM ** -0.5 * LOG2_E)
            o_ref[:, c0:c0 + LANES] = yc.astype(o_ref.dtype)
        off += width


def _project(x, mod, w, cos, sin, *, rows_per_mod, tm, rope_lanes, q_lanes, splits, name):
    t, d = x.shape
    n = w.shape[1]
    steps_per_mod = rows_per_mod // tm
    steps_per_seq = cos.shape[0] // tm
    kern = functools.partial(_proj_kernel, rope_lanes=rope_lanes, q_lanes=q_lanes, splits=splits)
    return pl.pallas_call(
        kern,
        grid=(t // tm,),
        in_specs=[
            pl.BlockSpec((tm, d), lambda i: (i, 0)),
            pl.BlockSpec((1, MOD_ROWS, d), lambda i: (i // steps_per_mod, 0, 0)),
            _const_spec((d, n)),
            pl.BlockSpec((tm, LANES), lambda i: (i % steps_per_seq, 0)),
            pl.BlockSpec((tm, LANES), lambda i: (i % steps_per_seq, 0)),
        ],
        out_specs=[pl.BlockSpec((tm, s), lambda i: (i, 0)) for s in splits],
        out_shape=[jax.ShapeDtypeStruct((t, s), BF16) for s in splits],
        compiler_params=_cparams(("parallel",)),
        name=name,
    )(x, mod, w, cos, sin)


def _attn_kernel(sink_ref, q_ref, km_ref, kp_ref, kn_ref, vm_ref, vp_ref, vn_ref, kvc_ref, o_ref, *, tq, seq, n_ctx):
    i = pl.program_id(1)
    nsub = tq // WINDOW
    span = 3 * WINDOW
    nkv = km_ref.shape[2] // LANES
    lane = lax.broadcasted_iota(jnp.int32, (1, LANES), 1)
    lo = lane < HEAD_DIM
    r = lax.broadcasted_iota(jnp.int32, (WINDOW, WINDOW), 0)
    c = lax.broadcasted_iota(jnp.int32, (WINDOW, WINDOW), 1)
    valid_first, valid_last = [], []
    for j in range(nsub):
        blk = i * (tq // WINDOW) + j
        valid_first.append((c >= r) & (blk > 0))
        valid_last.append((c <= r) & (blk < seq // WINDOW - 1))
    gidx = lax.broadcasted_iota(jnp.int32, (GQA_GROUP, 1, 1), 0)
    zero = jnp.zeros((), BF16)
    for kh in range(nkv):
        ks = slice(kh * LANES, (kh + 1) * LANES)
        kall = jnp.concatenate([kp_ref[0, :, ks], km_ref[0, :, ks], kn_ref[0, :, ks]], axis=0)
        vall = jnp.concatenate([vp_ref[0, :, ks], vm_ref[0, :, ks], vn_ref[0, :, ks]], axis=0)
        kc = kvc_ref[0, :, ks]
        vc = kvc_ref[0, :, nkv * LANES + kh * LANES:nkv * LANES + (kh + 1) * LANES]
        sink = jnp.full((GQA_GROUP, 1, 1), sink_ref[kh * GQA_GROUP], F32)
        for g in range(1, GQA_GROUP):
            sink = jnp.where(gidx == g, sink_ref[kh * GQA_GROUP + g], sink)
        sink = sink * LOG2_E
        for j in range(nsub):
            rows = slice(j * WINDOW, (j + 1) * WINDOW)
            kk = jnp.concatenate([kc, kall[j * WINDOW:j * WINDOW + span]], axis=0)
            vv = jnp.concatenate([vc, vall[j * WINDOW:j * WINDOW + span]], axis=0)
            parts = []
            for t in range(GQA_GROUP // 2):
                q2 = q_ref[0, rows, (2 * kh + t) * LANES:(2 * kh + t + 1) * LANES]
                parts += [jnp.where(lo, q2, zero), jnp.where(lo, zero, q2)]
            lhs = jnp.concatenate(parts, axis=0)
            s = lax.dot_general(lhs, kk, (((1,), (1,)), ((), ())), preferred_element_type=F32)
            s = s.reshape(GQA_GROUP, WINDOW, n_ctx + span)
            s = jnp.concatenate([
                s[:, :, :n_ctx],
                jnp.where(valid_first[j][None], s[:, :, n_ctx:n_ctx + WINDOW], MASK_VALUE),
                s[:, :, n_ctx + WINDOW:n_ctx + 2 * WINDOW],
                jnp.where(valid_last[j][None], s[:, :, n_ctx + 2 * WINDOW:], MASK_VALUE),
            ], axis=-1)
            m = jnp.maximum(jnp.max(s, axis=-1, keepdims=True), sink)
            p = jnp.exp2(s - m)
            denom = jnp.sum(p, axis=-1, keepdims=True) + jnp.exp2(sink - m)
            o = jnp.dot(p.astype(BF16).reshape(GQA_GROUP * WINDOW, n_ctx + span), vv, preferred_element_type=F32)
            o = o.reshape(GQA_GROUP, WINDOW, LANES) / denom
            for t in range(GQA_GROUP // 2):
                o2 = jnp.where(lo, o[2 * t], o[2 * t + 1])
                o_ref[0, rows, (2 * kh + t) * LANES:(2 * kh + t + 1) * LANES] = o2.astype(o_ref.dtype)


def _attention(q, kd, vd, kvc, sink, *, tq):
    b, s, qd = q.shape
    kvd = kd.shape[2]
    n_ctx = kvc.shape[1]
    nblk = s // WINDOW
    per = tq // WINDOW
    kern = functools.partial(_attn_kernel, tq=tq, seq=s, n_ctx=n_ctx)
    main = lambda bi, i: (bi, i, 0)
    prev = lambda bi, i: (bi, jnp.maximum(i * per - 1, 0), 0)
    nxt = lambda bi, i: (bi, jnp.minimum((i + 1) * per, nblk - 1), 0)
    return pl.pallas_call(
        kern,
        grid=(b, s // tq),
        in_specs=[
            pl.BlockSpec(memory_space=pltpu.SMEM),
            pl.BlockSpec((1, tq, qd), main),
            pl.BlockSpec((1, tq, kvd), main),
            pl.BlockSpec((1, WINDOW, kvd), prev),
            pl.BlockSpec((1, WINDOW, kvd), nxt),
            pl.BlockSpec((1, tq, kvd), main),
            pl.BlockSpec((1, WINDOW, kvd), prev),
            pl.BlockSpec((1, WINDOW, kvd), nxt),
            pl.BlockSpec((1, n_ctx, 2 * kvd), lambda bi, i: (bi, 0, 0)),
        ],
        out_specs=pl.BlockSpec((1, tq, qd), main),
        out_shape=jax.ShapeDtypeStruct((b, s, qd), BF16),
        compiler_params=_cparams(("parallel", "parallel")),
        name="window_attention",
    )(sink, q, kd, kd, kd, vd, vd, vd, kvc)


def _tail0_kernel(a_ref, x_ref, mod0_ref, mod1_ref, wo_ref, w1_ref, w3_ref, w2_ref, win_ref, ln_ref,
                  x_out, u_out, bg_out, *, fc):
    m0 = mod0_ref[0]
    m1 = mod1_ref[0]
    ln = ln_ref[...]
    d = x_ref.shape[1]
    mix = jnp.dot(a_ref[...], wo_ref[...], preferred_element_type=F32)
    x1 = _deepnorm_ln(x_ref[...], mix, m0[2:3, :], ln[0:1, :], ln[1:2, :])
    hb = (x1 * (1.0 + m0[4:5, :]) + m0[3:4, :]).astype(BF16)
    acc = jnp.zeros(x1.shape, F32)
    for c0 in range(0, w1_ref.shape[1], fc):
        a = jnp.dot(hb, w1_ref[:, c0:c0 + fc], preferred_element_type=F32)
        bb = jnp.dot(hb, w3_ref[:, c0:c0 + fc], preferred_element_type=F32)
        gact = (_silu(a) * bb).astype(BF16)
        acc = acc + jnp.dot(gact, w2_ref[c0:c0 + fc, :], preferred_element_type=F32)
    x2 = _deepnorm_ln(x1, acc, m0[5:6, :], ln[2:3, :], ln[3:4, :])
    x_out[...] = x2
    hb = (x2 * (1.0 + m1[1:2, :]) + m1[0:1, :]).astype(BF16)
    bg_out[...] = jnp.dot(hb, win_ref[:, :d], preferred_element_type=F32).astype(bg_out.dtype)
    cg = jnp.dot(hb, win_ref[:, d:2 * d], preferred_element_type=F32)
    val = jnp.dot(hb, win_ref[:, 2 * d:], preferred_element_type=F32)
    u_out[...] = (cg * val).astype(u_out.dtype)


def _layer0_tail(a, x, mod0, mod1, wo, w1, w3, w2, win, ln, *, rows_per_mod, tm, fc):
    t, d = x.shape
    spm = rows_per_mod // tm
    row = lambda i: (i, 0)
    modi = lambda i: (i // spm, 0, 0)
    return pl.pallas_call(
        functools.partial(_tail0_kernel, fc=fc),
        grid=(t // tm,),
        in_specs=[
            pl.BlockSpec((tm, a.shape[1]), row),
            pl.BlockSpec((tm, d), row),
            pl.BlockSpec((1, MOD_ROWS, d), modi),
            pl.BlockSpec((1, MOD_ROWS, d), modi),
            _const_spec(wo.shape),
            _const_spec(w1.shape),
            _const_spec(w3.shape),
            _const_spec(w2.shape),
            _const_spec(win.shape),
            _const_spec(ln.shape),
        ],
        out_specs=[pl.BlockSpec((tm, d), row)] * 3,
        out_shape=[jax.ShapeDtypeStruct((t, d), F32), jax.ShapeDtypeStruct((t, d), BF16),
                   jax.ShapeDtypeStruct((t, d), BF16)],
        compiler_params=_cparams(("parallel",)),
        name="oproj_ffn_convin",
    )(a, x, mod0, mod1, wo, w1, w3, w2, win, ln)


HALO = 16
RT_G1, RT_G2 = 8, 9
PLAN_ROWS = 8


def _convout_kernel(u_ref, up_ref, un_ref, bg_ref, cw_ref, w_ref, x_ref, mod_ref, g_ref, b_ref, wr_ref,
                    x_out, h_out, route_out, plan_out, cnt_out, carry_ref):
    j = pl.program_id(1)
    nj = pl.num_programs(1)

    @pl.when((pl.program_id(0) == 0) & (j == 0))
    def _():
        carry_ref[...] = jnp.zeros_like(carry_ref)

    m = mod_ref[0]
    u = u_ref[0].astype(F32)
    tm = u.shape[0]
    row = lax.broadcasted_iota(jnp.int32, (tm, 1), 0)
    prev_row = jnp.where(j == 0, 0.0, up_ref[0, HALO - 1:HALO, :].astype(F32))
    next_row = jnp.where(j == nj - 1, 0.0, un_ref[0, 0:1, :].astype(F32))
    u_prev = jnp.where(row == 0, prev_row, pltpu.roll(u, 1, 0))
    u_next = jnp.where(row == tm - 1, next_row, pltpu.roll(u, tm - 1, 0))
    cw = cw_ref[...]
    y = cw[0:1, :] * u_prev + cw[1:2, :] * u + cw[2:3, :] * u_next
    z = (bg_ref[0].astype(F32) * y).astype(BF16)
    mix = jnp.dot(z, w_ref[...], preferred_element_type=F32)
    x_new = _deepnorm_ln(x_ref[0], mix, m[2:3, :], g_ref[...], b_ref[...])
    x_out[0] = x_new
    h = x_new * (1.0 + m[4:5, :]) + m[3:4, :]
    for c in range(SUBLANES):
        h_out[0, c] = h[:, c * LANES:(c + 1) * LANES]
    hb = h.astype(BF16)
    lg = jnp.dot(hb, wr_ref[...], preferred_element_type=F32)
    lane = lax.broadcasted_iota(jnp.int32, lg.shape, 1).astype(F32)
    neg = jnp.float32(-jnp.inf)
    lg = jnp.where(lane < N_EXPERTS, lg, neg)
    m1 = jnp.max(lg, axis=-1, keepdims=True)
    i1 = jnp.min(jnp.where(lg == m1, lane, float(LANES)), axis=-1, keepdims=True)
    lg2 = jnp.where(lane == i1, neg, lg)
    m2 = jnp.max(lg2, axis=-1, keepdims=True)
    i2 = jnp.min(jnp.where(lg2 == m2, lane, float(LANES)), axis=-1, keepdims=True)
    e2 = jnp.exp(m2 - m1)
    gate1 = 1.0 / (1.0 + e2)
    gate2 = e2 / (1.0 + e2)
    member = jnp.where((lane == i1) | (lane == i2), 1.0, 0.0)
    route_out[0] = jnp.where(lane == RT_G1, gate1, jnp.where(lane == RT_G2, gate2, 0.0))
    r = lax.broadcasted_iota(jnp.int32, (tm, tm), 0)
    c = lax.broadcasted_iota(jnp.int32, (tm, tm), 1)
    tri = jnp.where(c <= r, 1.0, 0.0).astype(BF16)
    incl = jnp.dot(tri, member.astype(BF16), preferred_element_type=F32)
    before = incl - member + carry_ref[0:1, :]
    rank1 = jnp.sum(jnp.where(lane == i1, before, 0.0), axis=-1, keepdims=True)
    rank2 = jnp.sum(jnp.where(lane == i2, before, 0.0), axis=-1, keepdims=True)
    rec = jnp.where(lane == 0, rank1, jnp.where(lane == 1, rank2,
                    jnp.where(lane == 2, i1, jnp.where(lane == 3, i2, 0.0))))
    plan_out[...] = rec.T[:PLAN_ROWS, :].astype(jnp.int32)
    carry_ref[0:1, :] = carry_ref[0:1, :] + incl[tm - 1:tm, :]
    cnt_out[...] = carry_ref[...]


def _conv_out(u, bg, cw, w, x, mod, g, b, wr, *, tm):
    bsz, s, d = x.shape
    per = tm // HALO
    nh = s // HALO
    nj = s // tm
    main = lambda bi, j: (bi, j, 0)
    return pl.pallas_call(
        _convout_kernel,
        grid=(bsz, s // tm),
        in_specs=[
            pl.BlockSpec((1, tm, d), main),
            pl.BlockSpec((1, HALO, d), lambda bi, j: (bi, jnp.maximum(j * per - 1, 0), 0)),
            pl.BlockSpec((1, HALO, d), lambda bi, j: (bi, jnp.minimum((j + 1) * per, nh - 1), 0)),
            pl.BlockSpec((1, tm, d), main),
            _const_spec(cw.shape),
            _const_spec(w.shape),
            pl.BlockSpec((1, tm, d), main),
            pl.BlockSpec((1, MOD_ROWS, d), lambda bi, j: (bi, 0, 0)),
            _const_spec((1, d)),
            _const_spec((1, d)),
            _const_spec(wr.shape),
        ],
        out_specs=[
            pl.BlockSpec((1, tm, d), main),
            pl.BlockSpec((1, SUBLANES, tm, d // SUBLANES), lambda bi, j: (bi, 0, j, 0)),
            pl.BlockSpec((1, tm, LANES), main),
            pl.BlockSpec((PLAN_ROWS, tm), lambda bi, j: (0, bi * nj + j)),
            pl.BlockSpec((8, LANES), lambda bi, j: (0, 0)),
        ],
        out_shape=[
            jax.ShapeDtypeStruct((bsz, s, d), F32),
            jax.ShapeDtypeStruct((bsz, SUBLANES, s, d // SUBLANES), F32),
            jax.ShapeDtypeStruct((bsz, s, LANES), F32),
            jax.ShapeDtypeStruct((PLAN_ROWS, bsz * s), jnp.int32),
            jax.ShapeDtypeStruct((8, LANES), F32),
        ],
        scratch_shapes=[pltpu.VMEM((8, LANES), F32)],
        compiler_params=_cparams(("arbitrary", "arbitrary")),
        name="conv_out_ln_router",
    )(u, u, u, bg, cw, w, x, mod, g, b, wr)


MOE_TILE = 512


def _row_copy(src_ref, src_row, dst_ref, dst_row, sem):
    return pltpu.make_async_copy(src_ref.at[src_row], dst_ref.at[dst_row], sem)


def _chunked_row(ref, row):
    return ref.at[:, row, :]


def _chunk_copies(chunked_ref, tiled_ref, row0, sem, *, to_tiled):
    rows = chunked_ref.shape[1]
    out = []
    for c in range(SUBLANES):
        hbm = tiled_ref.at[pl.ds(row0, rows), c, :]
        vmem = chunked_ref.at[c]
        out.append(pltpu.make_async_copy(vmem, hbm, sem) if to_tiled else pltpu.make_async_copy(hbm, vmem, sem))
    return out


DMA_UNROLL = 8


def _dispatch_kernel(d1_ref, d2_ref, padlo_ref, padhi_ref, na_ref, h_ref, w1_ref, w3_ref, w2_ref,
                     xs_ref, w1_out, w3_out, w2_out, zero_ref, sem, sem2, zsem, *, tile):
    i = pl.program_id(0)
    tm = h_ref.shape[2]
    hsrc = h_ref.at[0]
    n_tiles = xs_ref.shape[0] // tile

    @pl.when(i == 0)
    def _():
        zero_ref[...] = jnp.zeros_like(zero_ref)

        def zrow_start(r, carry):
            _row_copy(zero_ref, 0, xs_ref, r, zsem).start()
            return carry

        def zrow_wait(r, carry):
            _row_copy(zero_ref, 0, xs_ref, r, zsem).wait()
            return carry

        def ztile_start(k, carry):
            pltpu.make_async_copy(zero_ref, xs_ref.at[pl.ds(k * tile, tile)], zsem).start()
            return carry

        def ztile_wait(k, carry):
            pltpu.make_async_copy(zero_ref, xs_ref.at[pl.ds(k * tile, tile)], zsem).wait()
            return carry

        for e in range(N_EXPERTS):
            lax.fori_loop(padlo_ref[e], padhi_ref[e], zrow_start, 0)
            lax.fori_loop(padlo_ref[e], padhi_ref[e], zrow_wait, 0)
        lax.fori_loop(na_ref[0], n_tiles, ztile_start, 0)
        lax.fori_loop(na_ref[0], n_tiles, ztile_wait, 0)

    base = i * tm

    def start(r, carry):
        pltpu.make_async_copy(_chunked_row(hsrc, r), xs_ref.at[d1_ref[base + r]], sem).start(priority=0)
        pltpu.make_async_copy(_chunked_row(hsrc, r), xs_ref.at[d2_ref[base + r]], sem2).start(priority=1)
        return carry

    def wait(r, carry):
        pltpu.make_async_copy(_chunked_row(hsrc, r), xs_ref.at[d1_ref[base + r]], sem).wait()
        pltpu.make_async_copy(_chunked_row(hsrc, r), xs_ref.at[d2_ref[base + r]], sem2).wait()
        return carry

    lax.fori_loop(0, tm, start, 0, unroll=DMA_UNROLL)
    w1_out[...] = w1_ref[...].astype(w1_out.dtype)
    w3_out[...] = w3_ref[...].astype(w3_out.dtype)
    w2_out[...] = w2_ref[...].astype(w2_out.dtype)
    lax.fori_loop(0, tm, wait, 0, unroll=DMA_UNROLL)


def _moe_dispatch(h, dest1, dest2, pad_lo, pad_hi, n_active, w1, w3, w2, *, n_rows, tm, tile):
    bsz, nch, seq, lanes = h.shape
    per = seq // tm
    steps = bsz * per
    ne, d, ff = w1.shape
    w13 = (ne * d // steps, ff)
    w2s = (ne * ff // steps, d)
    smem = pl.BlockSpec(memory_space=pltpu.SMEM)
    dma = pltpu.SemaphoreType.DMA(())
    rows = lambda i: (i, 0)
    xs, w1b, w3b, w2b = pl.pallas_call(
        functools.partial(_dispatch_kernel, tile=tile),
        grid=(steps,),
        in_specs=[smem, smem, smem, smem, smem,
                  pl.BlockSpec((1, nch, tm, lanes), lambda i: (i // per, 0, i % per, 0)),
                  pl.BlockSpec(w13, rows), pl.BlockSpec(w13, rows), pl.BlockSpec(w2s, rows)],
        out_specs=[pl.BlockSpec(memory_space=pl.ANY),
                   pl.BlockSpec(w13, rows), pl.BlockSpec(w13, rows), pl.BlockSpec(w2s, rows)],
        out_shape=[jax.ShapeDtypeStruct((n_rows, nch, lanes), F32),
                   jax.ShapeDtypeStruct((ne * d, ff), BF16), jax.ShapeDtypeStruct((ne * d, ff), BF16),
                   jax.ShapeDtypeStruct((ne * ff, d), BF16)],
        scratch_shapes=[pltpu.VMEM((tile, nch, lanes), F32), dma, dma, dma],
        compiler_params=_cparams(("arbitrary",)),
        name="moe_dispatch",
    )(dest1, dest2, pad_lo, pad_hi, n_active, h, w1.reshape(ne * d, ff), w3.reshape(ne * d, ff),
      w2.reshape(ne * ff, d))
    return xs, w1b.reshape(ne, d, ff), w3b.reshape(ne, d, ff), w2b.reshape(ne, ff, d)


def _expert_kernel(te_ref, na_ref, xs_ref, w1_ref, w3_ref, w2_ref, ys_ref, xbuf, obuf, xsem, osem, *, fc):
    i = pl.program_id(0)
    n = pl.num_programs(0)
    na = na_ref[0]
    tm = xbuf.shape[2]
    slot = i % 2

    def fetch(k):
        return _chunk_copies(xbuf.at[k % 2], xs_ref, k * tm, xsem.at[k % 2], to_tiled=False)

    def writeback(k):
        return _chunk_copies(obuf.at[k % 2], ys_ref, k * tm, osem.at[k % 2], to_tiled=True)

    @pl.when(i == 0)
    def _():
        for cp in fetch(i):
            cp.start()

    @pl.when(i + 1 < na)
    def _():
        for cp in fetch(i + 1):
            cp.start()

    @pl.when(i >= 2)
    def _():
        for cp in writeback(i - 2):
            cp.wait()

    @pl.when(i < na)
    def _():
        for cp in fetch(i):
            cp.wait()
        xb = jnp.concatenate([xbuf[slot, c] for c in range(SUBLANES)], axis=-1).astype(BF16)
        acc = jnp.zeros(xb.shape, F32)
        for c0 in range(0, w1_ref.shape[2], fc):
            a = jnp.dot(xb, w1_ref[0, :, c0:c0 + fc], preferred_element_type=F32)
            bb = jnp.dot(xb, w3_ref[0, :, c0:c0 + fc], preferred_element_type=F32)
            gact = (_silu(a) * bb).astype(BF16)
            acc = acc + jnp.dot(gact, w2_ref[0, c0:c0 + fc, :], preferred_element_type=F32)
        for c in range(SUBLANES):
            obuf[slot, c] = acc[:, c * LANES:(c + 1) * LANES]

    @pl.when(i >= na)
    def _():
        obuf[slot] = jnp.zeros(obuf.shape[1:], obuf.dtype)

    for cp in writeback(i):
        cp.start()

    @pl.when(i == n - 1)
    def _():
        for cp in writeback(i):
            cp.wait()

        @pl.when(i >= 1)
        def _():
            for cp in writeback(i - 1):
                cp.wait()


def _moe_experts(xs, tile_expert, n_active, w1, w3, w2, *, tm, fc):
    n_rows, nch, lanes = xs.shape
    _, d, ff = w1.shape

    def wspec(shape):
        return pl.BlockSpec(shape, lambda i, te, na: (te[i], 0, 0))

    grid_spec = pltpu.PrefetchScalarGridSpec(
        num_scalar_prefetch=2,
        grid=(n_rows // tm,),
        in_specs=[pl.BlockSpec(memory_space=pl.ANY), wspec((1, d, ff)), wspec((1, d, ff)), wspec((1, ff, d))],
        out_specs=pl.BlockSpec(memory_space=pl.ANY),
        scratch_shapes=[pltpu.VMEM((2, nch, tm, lanes), F32), pltpu.VMEM((2, nch, tm, lanes), F32),
                        pltpu.SemaphoreType.DMA((2,)), pltpu.SemaphoreType.DMA((2,))],
    )
    return pl.pallas_call(
        functools.partial(_expert_kernel, fc=fc),
        grid_spec=grid_spec,
        out_shape=jax.ShapeDtypeStruct((n_rows, nch, lanes), F32),
        compiler_params=_cparams(("arbitrary",)),
        name="moe_experts",
    )(tile_expert, n_active, xs, w1, w3, w2)


def _combine_kernel(d1_ref, d2_ref, ys_ref, rt_ref, x_ref, mod_ref, g_ref, b_ref, o_ref, y1_ref, y2_ref, sem, sem2):
    i = pl.program_id(0)
    tm = x_ref.shape[0]
    base = i * tm

    def start(r, carry):
        pltpu.make_async_copy(ys_ref.at[d1_ref[base + r]], _chunked_row(y1_ref, r), sem).start(priority=0)
        pltpu.make_async_copy(ys_ref.at[d2_ref[base + r]], _chunked_row(y2_ref, r), sem2).start(priority=1)
        return carry

    def wait(r, carry):
        pltpu.make_async_copy(ys_ref.at[d1_ref[base + r]], _chunked_row(y1_ref, r), sem).wait()
        pltpu.make_async_copy(ys_ref.at[d2_ref[base + r]], _chunked_row(y2_ref, r), sem2).wait()
        return carry

    lax.fori_loop(0, tm, start, 0, unroll=DMA_UNROLL)
    lax.fori_loop(0, tm, wait, 0, unroll=DMA_UNROLL)
    rt = rt_ref[...]
    y1 = jnp.concatenate([y1_ref[c] for c in range(SUBLANES)], axis=-1)
    y2 = jnp.concatenate([y2_ref[c] for c in range(SUBLANES)], axis=-1)
    y = rt[:, RT_G1:RT_G1 + 1] * y1 + rt[:, RT_G2:RT_G2 + 1] * y2
    o_ref[...] = _deepnorm_ln(x_ref[...], y, mod_ref[0][5:6, :], g_ref[...], b_ref[...])


def _moe_combine_ln(ys, dest1, dest2, rt, x, mod, g, b, *, rows_per_mod, tm):
    t, d = x.shape
    spm = rows_per_mod // tm
    smem = pl.BlockSpec(memory_space=pltpu.SMEM)
    dma = pltpu.SemaphoreType.DMA(())
    return pl.pallas_call(
        _combine_kernel,
        grid=(t // tm,),
        in_specs=[
            smem, smem,
            pl.BlockSpec(memory_space=pl.ANY),
            pl.BlockSpec((tm, LANES), lambda i: (i, 0)),
            pl.BlockSpec((tm, d), lambda i: (i, 0)),
            pl.BlockSpec((1, MOD_ROWS, d), lambda i: (i // spm, 0, 0)),
            _const_spec((1, d)),
            _const_spec((1, d)),
        ],
        out_specs=pl.BlockSpec((tm, d), lambda i: (i, 0)),
        out_shape=jax.ShapeDtypeStruct((t, d), F32),
        scratch_shapes=[pltpu.VMEM((SUBLANES, tm, LANES), F32), pltpu.VMEM((SUBLANES, tm, LANES), F32), dma, dma],
        compiler_params=_cparams(("arbitrary",)),
        name="moe_combine_ln",
    )(dest1, dest2, ys, rt, x, mod, g, b)


def _moe_layout(plan, counts, *, tile):
    n_pairs = plan.shape[1] * 2
    n_tiles = n_pairs // tile + N_EXPERTS
    cnt = counts[0, :N_EXPERTS].astype(jnp.int32)
    padded = (cnt + tile - 1) // tile * tile
    ends = jnp.cumsum(padded)
    starts = ends - padded
    dest1 = starts[plan[2]] + plan[0]
    dest2 = starts[plan[3]] + plan[1]
    n_active = ends[-1:] // tile
    tiles = jnp.minimum(jnp.arange(n_tiles, dtype=jnp.int32), n_active[0] - 1) * tile
    tile_expert = jnp.sum((ends[None, :] <= tiles[:, None]).astype(jnp.int32), axis=1)
    tile_expert = jnp.minimum(tile_expert, N_EXPERTS - 1)
    return dest1, dest2, starts + cnt, ends, tile_expert, n_active.astype(jnp.int32), n_tiles * tile


def _rope_tables(rows_count):
    half = HEAD_DIM // 2
    rows = jnp.repeat(jnp.arange(rows_count, dtype=F32), GRID_W)
    cols = jnp.tile(jnp.arange(GRID_W, dtype=F32), rows_count)
    inv_freq = ROPE_THETA ** (-jnp.arange(0, half, 2, dtype=F32) / half)
    ar = rows[:, None] * inv_freq
    ac = cols[:, None] * inv_freq
    ang = jnp.concatenate([ar, ar, ac, ac], axis=-1)
    sign = jnp.where((jnp.arange(HEAD_DIM) % 32) < 16, -1.0, 1.0).astype(F32)
    cos = jnp.tile(jnp.cos(ang), (1, LANES // HEAD_DIM))
    sin = jnp.tile(jnp.sin(ang) * sign, (1, LANES // HEAD_DIM))
    return cos, sin


def _dup_heads(w, n_heads):
    d = w.shape[0]
    w = w.reshape(d, n_heads, 1, HEAD_DIM)
    return jnp.broadcast_to(w, (d, n_heads, 2, HEAD_DIM)).reshape(d, n_heads * 2 * HEAD_DIM)


def _mod_blocks(mod_rows, d):
    r = mod_rows.shape[0]
    m = mod_rows.reshape(r, 6, d)
    return jnp.concatenate([m, jnp.zeros((r, MOD_ROWS - 6, d), F32)], axis=1)


def kernel(x, c, ctx, c_ctx, w_mod, b_mod, ln_g, ln_b, attn_w_qkv, attn_w_o, attn_sink,
           conv_w_in, conv_w, conv_w_out, ffn_w1, ffn_w3, ffn_w2,
           moe_router, moe_w1, moe_w3, moe_w2):
    bsz, seq, d = x.shape
    n_ctx = ctx.shape[1]
    t = bsz * seq
    q_dim = attn_w_o.shape[1]
    kv_dim = (attn_w_qkv.shape[2] - q_dim) // 2
    n_kv = kv_dim // HEAD_DIM

    pad_rows = (-(bsz + 1)) % 8
    cvec = jnp.concatenate([c, c_ctx[None, :], jnp.zeros((pad_rows, d), F32)], axis=0)
    mod_all = _modulation(cvec, w_mod, b_mod)
    mod0 = _mod_blocks(mod_all[0, :bsz], d)
    modc0 = _mod_blocks(mod_all[0, bsz:bsz + 1], d)
    mod1 = _mod_blocks(mod_all[1, :bsz], d)

    cos, sin = _rope_tables(seq // GRID_W)

    wqkv = attn_w_qkv[0]
    wk = _dup_heads(wqkv[:, q_dim:q_dim + kv_dim], n_kv)
    wv = _dup_heads(wqkv[:, q_dim + kv_dim:], n_kv)
    w_all = jnp.concatenate([wqkv[:, :q_dim], wk, wv], axis=1).astype(BF16)
    w_kv = jnp.concatenate([wk, wv], axis=1).astype(BF16)
    x2d = x.reshape(t, d)
    q, kd, vd = _project(x2d, mod0, w_all, cos, sin, rows_per_mod=seq, tm=512,
                         rope_lanes=q_dim + 2 * kv_dim, q_lanes=q_dim,
                         splits=(q_dim, 2 * kv_dim, 2 * kv_dim), name="qkv_rope")
    (kvc,) = _project(ctx.reshape(bsz * n_ctx, d), modc0, w_kv, cos, sin, rows_per_mod=bsz * n_ctx, tm=512,
                      rope_lanes=0, q_lanes=0, splits=(4 * kv_dim,), name="ctx_kv")
    attn = _attention(q.reshape(bsz, seq, q_dim), kd.reshape(bsz, seq, 2 * kv_dim), vd.reshape(bsz, seq, 2 * kv_dim),
                      kvc.reshape(bsz, n_ctx, 4 * kv_dim), attn_sink[0], tq=256)
    ln0 = jnp.concatenate([ln_g[0, 0][None], ln_b[0, 0][None], ln_g[0, 1][None], ln_b[0, 1][None],
                           jnp.zeros((4, d), F32)], axis=0)
    x2, u, bg = _layer0_tail(attn.reshape(t, q_dim), x2d, mod0, mod1, attn_w_o[0].astype(BF16),
                             ffn_w1[0].astype(BF16), ffn_w3[0].astype(BF16), ffn_w2[0].astype(BF16),
                             conv_w_in[0].astype(BF16), ln0, rows_per_mod=seq, tm=512, fc=256)

    cw = jnp.concatenate([conv_w[0], jnp.zeros((8 - CONV_WIDTH, d), F32)], axis=0)
    wr = jnp.concatenate([moe_router[0], jnp.zeros((d, LANES - N_EXPERTS), F32)], axis=1).astype(BF16)
    x3, h4, rt, plan, counts = _conv_out(u.reshape(bsz, seq, d), bg.reshape(bsz, seq, d), cw,
                                         conv_w_out[0].astype(BF16), x2.reshape(bsz, seq, d), mod1,
                                         ln_g[1, 0][None], ln_b[1, 0][None], wr, tm=512)
    rt = rt.reshape(t, LANES)
    dest1, dest2, pad_lo, pad_hi, tile_expert, n_active, n_rows = _moe_layout(plan, counts, tile=MOE_TILE)
    xs, w1b, w3b, w2b = _moe_dispatch(h4, dest1, dest2, pad_lo, pad_hi, n_active, moe_w1[0], moe_w3[0], moe_w2[0],
                                      n_rows=n_rows, tm=512, tile=MOE_TILE)
    ys = _moe_experts(xs, tile_expert, n_active, w1b, w3b, w2b, tm=MOE_TILE, fc=512)
    out = _moe_combine_ln(ys, dest1, dest2, rt, x3.reshape(t, d), mod1, ln_g[1, 1][None], ln_b[1, 1][None],
                          rows_per_mod=seq, tm=512)
    return out.reshape(bsz, seq, d)
```

```python
import functools

import jax
import jax.numpy as jnp
from jax import lax
from jax.experimental import pallas as pl
from jax.experimental.pallas import tpu as pltpu

F32 = jnp.float32
BF16 = jnp.bfloat16

GRID_W = 64
HEAD_DIM = 64
GQA_GROUP = 4
WINDOW = 128
ROPE_THETA = 10000.0
CONV_WIDTH = 3
N_EXPERTS = 8
LN_EPS = 1e-5
MASK_VALUE = -1e30
DEPTH = 2
DEEPNORM_ALPHA = (2 * DEPTH) ** 0.25
LOG2_E = 1.4426950408889634

LANES = 128
MOD_ROWS = 8
VMEM_LIMIT = 56 * 1024 * 1024


def _cparams(sem):
    return pltpu.CompilerParams(dimension_semantics=sem, vmem_limit_bytes=VMEM_LIMIT)


def _const_spec(shape):
    nd = len(shape)
    return pl.BlockSpec(shape, lambda *_: (0,) * nd, pipeline_mode=pl.Buffered(1))


def _silu(a):
    return a / (1.0 + jnp.exp(-a))


SUBLANES = 8


def _deepnorm_ln(x, mix, gate, g, b):
    y = DEEPNORM_ALPHA * x + gate * mix
    mu = jnp.mean(y, axis=-1, keepdims=True)
    d = y - mu
    var = jnp.mean(d * d, axis=-1, keepdims=True)
    return d * lax.rsqrt(var + LN_EPS) * g + b


def _mod_kernel(c_ref, w_ref, b_ref, o_ref):
    cv = c_ref[...]
    s = _silu(cv).astype(BF16)
    o_ref[0] = jnp.dot(s, w_ref[0].astype(BF16), preferred_element_type=F32) + b_ref[0]


def _modulation(cvec, w_mod, b_mod):
    depth, d, n = w_mod.shape
    rows = cvec.shape[0]
    nt = 1536
    return pl.pallas_call(
        _mod_kernel,
        grid=(depth, n // nt),
        in_specs=[
            pl.BlockSpec((rows, d), lambda l, j: (0, 0)),
            pl.BlockSpec((1, d, nt), lambda l, j: (l, 0, j)),
            pl.BlockSpec((1, 1, nt), lambda l, j: (l, 0, j)),
        ],
        out_specs=pl.BlockSpec((1, rows, nt), lambda l, j: (l, 0, j)),
        out_shape=jax.ShapeDtypeStruct((depth, rows, n), F32),
        compiler_params=_cparams(("parallel", "parallel")),
        name="modulation",
    )(cvec, w_mod, b_mod.reshape(depth, 1, n))


def _proj_kernel(x_ref, mod_ref, w_ref, cos_ref, sin_ref, *o_refs, rope_lanes, q_lanes, splits):
    m = mod_ref[0]
    h = x_ref[...] * (1.0 + m[1:2, :]) + m[0:1, :]
    y = jnp.dot(h.astype(BF16), w_ref[...], preferred_element_type=F32)
    if rope_lanes:
        cos = cos_ref[...]
        sin = sin_ref[...]
        first_half = (lax.broadcasted_iota(jnp.int32, (1, LANES), 1) % 32) < 16
    off = 0
    for o_ref, width in zip(o_refs, splits):
        for c0 in range(0, width, LANES):
            yc = y[:, off + c0:off + c0 + LANES]
            if off + c0 < rope_lanes:
                rot = jnp.where(first_half, pltpu.roll(yc, LANES - 16, 1), pltpu.roll(yc, 16, 1))
                yc = yc * cos + rot * sin
                if off + c0 < q_lanes:
                    yc = yc * (HEAD_DIM ** -0.5 * LOG2_E)
            o_ref[:, c0:c0 + LANES] = yc.astype(o_ref.dtype)
        off += width


def _project(x, mod, w, cos, sin, *, rows_per_mod, tm, rope_lanes, q_lanes, splits, name):
    t, d = x.shape
    n = w.shape[1]
    steps_per_mod = rows_per_mod // tm
    steps_per_seq = cos.shape[0] // tm
    kern = functools.partial(_proj_kernel, rope_lanes=rope_lanes, q_lanes=q_lanes, splits=splits)
    return pl.pallas_call(
        kern,
        grid=(t // tm,),
        in_specs=[
            pl.BlockSpec((tm, d), lambda i: (i, 0)),
            pl.BlockSpec((1, MOD_ROWS, d), lambda i: (i // steps_per_mod, 0, 0)),
            _const_spec((d, n)),
            pl.BlockSpec((tm, LANES), lambda i: (i % steps_per_seq, 0)),
            pl.BlockSpec((tm, LANES), lambda i: (i % steps_per_seq, 0)),
        ],
        out_specs=[pl.BlockSpec((tm, s), lambda i: (i, 0)) for s in splits],
        out_shape=[jax.ShapeDtypeStruct((t, s), BF16) for s in splits],
        compiler_params=_cparams(("parallel",)),
        name=name,
    )(x, mod, w, cos, sin)


def _attn_kernel(sink_ref, q_ref, km_ref, kp_ref, kn_ref, vm_ref, vp_ref, vn_ref, kvc_ref, w1_ref, w3_ref, w2_ref,
                 o_ref, w1_out, w3_out, w2_out, *, tq, seq, n_ctx):
    w1_out[...] = w1_ref[...].astype(w1_out.dtype)
    w3_out[...] = w3_ref[...].astype(w3_out.dtype)
    w2_out[...] = w2_ref[...].astype(w2_out.dtype)
    i = pl.program_id(1)
    nsub = tq // WINDOW
    span = 3 * WINDOW
    nkv = km_ref.shape[2] // LANES
    lane = lax.broadcasted_iota(jnp.int32, (1, LANES), 1)
    lo = lane < HEAD_DIM
    r = lax.broadcasted_iota(jnp.int32, (WINDOW, WINDOW), 0)
    c = lax.broadcasted_iota(jnp.int32, (WINDOW, WINDOW), 1)
    valid_first, valid_last = [], []
    for j in range(nsub):
        blk = i * (tq // WINDOW) + j
        valid_first.append((c >= r) & (blk > 0))
        valid_last.append((c <= r) & (blk < seq // WINDOW - 1))
    gidx = lax.broadcasted_iota(jnp.int32, (GQA_GROUP, 1, 1), 0)
    zero = jnp.zeros((), BF16)
    for kh in range(nkv):
        ks = slice(kh * LANES, (kh + 1) * LANES)
        kall = jnp.concatenate([kp_ref[0, :, ks], km_ref[0, :, ks], kn_ref[0, :, ks]], axis=0)
        vall = jnp.concatenate([vp_ref[0, :, ks], vm_ref[0, :, ks], vn_ref[0, :, ks]], axis=0)
        kc = kvc_ref[0, :, ks]
        vc = kvc_ref[0, :, nkv * LANES + kh * LANES:nkv * LANES + (kh + 1) * LANES]
        sink = jnp.full((GQA_GROUP, 1, 1), sink_ref[kh * GQA_GROUP], F32)
        for g in range(1, GQA_GROUP):
            sink = jnp.where(gidx == g, sink_ref[kh * GQA_GROUP + g], sink)
        sink = sink * LOG2_E
        for j in range(nsub):
            rows = slice(j * WINDOW, (j + 1) * WINDOW)
            kk = jnp.concatenate([kc, kall[j * WINDOW:j * WINDOW + span]], axis=0)
            vv = jnp.concatenate([vc, vall[j * WINDOW:j * WINDOW + span]], axis=0)
            parts = []
            for t in range(GQA_GROUP // 2):
                q2 = q_ref[0, rows, (2 * kh + t) * LANES:(2 * kh + t + 1) * LANES]
                parts += [jnp.where(lo, q2, zero), jnp.where(lo, zero, q2)]
            lhs = jnp.concatenate(parts, axis=0)
            s = lax.dot_general(lhs, kk, (((1,), (1,)), ((), ())), preferred_element_type=F32)
            s = s.reshape(GQA_GROUP, WINDOW, n_ctx + span)
            s = jnp.concatenate([
                s[:, :, :n_ctx],
                jnp.where(valid_first[j][None], s[:, :, n_ctx:n_ctx + WINDOW], MASK_VALUE),
                s[:, :, n_ctx + WINDOW:n_ctx + 2 * WINDOW],
                jnp.where(valid_last[j][None], s[:, :, n_ctx + 2 * WINDOW:], MASK_VALUE),
            ], axis=-1)
            m = jnp.maximum(jnp.max(s, axis=-1, keepdims=True), sink)
            p = jnp.exp2(s - m)
            denom = jnp.sum(p, axis=-1, keepdims=True) + jnp.exp2(sink - m)
            o = jnp.dot(p.astype(BF16).reshape(GQA_GROUP * WINDOW, n_ctx + span), vv, preferred_element_type=F32)
            o = o.reshape(GQA_GROUP, WINDOW, LANES) / denom
            for t in range(GQA_GROUP // 2):
                o2 = jnp.where(lo, o[2 * t], o[2 * t + 1])
                o_ref[0, rows, (2 * kh + t) * LANES:(2 * kh + t + 1) * LANES] = o2.astype(o_ref.dtype)


def _attention(q, kd, vd, kvc, sink, w1, w3, w2, *, tq):
    b, s, qd = q.shape
    kvd = kd.shape[2]
    n_ctx = kvc.shape[1]
    nblk = s // WINDOW
    per = tq // WINDOW
    nq = s // tq
    steps = b * nq
    ne, d, ff = w1.shape
    w13 = (ne * d // steps, ff)
    w2s = (ne * ff // steps, d)
    kern = functools.partial(_attn_kernel, tq=tq, seq=s, n_ctx=n_ctx)
    main = lambda bi, i: (bi, i, 0)
    prev = lambda bi, i: (bi, jnp.maximum(i * per - 1, 0), 0)
    nxt = lambda bi, i: (bi, jnp.minimum((i + 1) * per, nblk - 1), 0)
    wrows = lambda bi, i: (bi * nq + i, 0)
    out, w1b, w3b, w2b = pl.pallas_call(
        kern,
        grid=(b, nq),
        in_specs=[
            pl.BlockSpec(memory_space=pltpu.SMEM),
            pl.BlockSpec((1, tq, qd), main),
            pl.BlockSpec((1, tq, kvd), main),
            pl.BlockSpec((1, WINDOW, kvd), prev),
            pl.BlockSpec((1, WINDOW, kvd), nxt),
            pl.BlockSpec((1, tq, kvd), main),
            pl.BlockSpec((1, WINDOW, kvd), prev),
            pl.BlockSpec((1, WINDOW, kvd), nxt),
            pl.BlockSpec((1, n_ctx, 2 * kvd), lambda bi, i: (bi, 0, 0)),
            pl.BlockSpec(w13, wrows), pl.BlockSpec(w13, wrows), pl.BlockSpec(w2s, wrows),
        ],
        out_specs=[pl.BlockSpec((1, tq, qd), main),
                   pl.BlockSpec(w13, wrows), pl.BlockSpec(w13, wrows), pl.BlockSpec(w2s, wrows)],
        out_shape=[jax.ShapeDtypeStruct((b, s, qd), BF16),
                   jax.ShapeDtypeStruct((ne * d, ff), BF16), jax.ShapeDtypeStruct((ne * d, ff), BF16),
                   jax.ShapeDtypeStruct((ne * ff, d), BF16)],
        compiler_params=_cparams(("parallel", "parallel")),
        name="window_attention",
    )(sink, q, kd, kd, kd, vd, vd, vd, kvc, w1.reshape(ne * d, ff), w3.reshape(ne * d, ff),
      w2.reshape(ne * ff, d))
    return out, w1b.reshape(ne, d, ff), w3b.reshape(ne, d, ff), w2b.reshape(ne, ff, d)


def _tail0_kernel(a_ref, x_ref, mod0_ref, mod1_ref, wo_ref, w1_ref, w3_ref, w2_ref, win_ref, ln_ref,
                  x_out, u_out, bg_out, *, fc):
    m0 = mod0_ref[0]
    m1 = mod1_ref[0]
    ln = ln_ref[...]
    d = x_ref.shape[1]
    mix = jnp.dot(a_ref[...], wo_ref[...], preferred_element_type=F32)
    x1 = _deepnorm_ln(x_ref[...], mix, m0[2:3, :], ln[0:1, :], ln[1:2, :])
    hb = (x1 * (1.0 + m0[4:5, :]) + m0[3:4, :]).astype(BF16)
    acc = jnp.zeros(x1.shape, F32)
    for c0 in range(0, w1_ref.shape[1], fc):
        a = jnp.dot(hb, w1_ref[:, c0:c0 + fc], preferred_element_type=F32)
        bb = jnp.dot(hb, w3_ref[:, c0:c0 + fc], preferred_element_type=F32)
        gact = (_silu(a) * bb).astype(BF16)
        acc = acc + jnp.dot(gact, w2_ref[c0:c0 + fc, :], preferred_element_type=F32)
    x2 = _deepnorm_ln(x1, acc, m0[5:6, :], ln[2:3, :], ln[3:4, :])
    x_out[...] = x2
    hb = (x2 * (1.0 + m1[1:2, :]) + m1[0:1, :]).astype(BF16)
    bg_out[...] = jnp.dot(hb, win_ref[:, :d], preferred_element_type=F32).astype(bg_out.dtype)
    cg = jnp.dot(hb, win_ref[:, d:2 * d], preferred_element_type=F32)
    val = jnp.dot(hb, win_ref[:, 2 * d:], preferred_element_type=F32)
    u_out[...] = (cg * val).astype(u_out.dtype)


def _layer0_tail(a, x, mod0, mod1, wo, w1, w3, w2, win, ln, *, rows_per_mod, tm, fc):
    t, d = x.shape
    spm = rows_per_mod // tm
    row = lambda i: (i, 0)
    modi = lambda i: (i // spm, 0, 0)
    return pl.pallas_call(
        functools.partial(_tail0_kernel, fc=fc),
        grid=(t // tm,),
        in_specs=[
            pl.BlockSpec((tm, a.shape[1]), row),
            pl.BlockSpec((tm, d), row),
            pl.BlockSpec((1, MOD_ROWS, d), modi),
            pl.BlockSpec((1, MOD_ROWS, d), modi),
            _const_spec(wo.shape),
            _const_spec(w1.shape),
            _const_spec(w3.shape),
            _const_spec(w2.shape),
            _const_spec(win.shape),
            _const_spec(ln.shape),
        ],
        out_specs=[pl.BlockSpec((tm, d), row)] * 3,
        out_shape=[jax.ShapeDtypeStruct((t, d), F32), jax.ShapeDtypeStruct((t, d), BF16),
                   jax.ShapeDtypeStruct((t, d), BF16)],
        compiler_params=_cparams(("parallel",)),
        name="oproj_ffn_convin",
    )(a, x, mod0, mod1, wo, w1, w3, w2, win, ln)


HALO = 16
RT_G1, RT_G2 = 8, 9
PLAN_ROWS = 8


def _convout_kernel(u_ref, up_ref, un_ref, bg_ref, cw_ref, w_ref, x_ref, mod_ref, g_ref, b_ref, wr_ref,
                    x_out, h_out, route_out, plan_out, cnt_out, carry_ref):
    j = pl.program_id(1)
    nj = pl.num_programs(1)

    @pl.when((pl.program_id(0) == 0) & (j == 0))
    def _():
        carry_ref[...] = jnp.zeros_like(carry_ref)

    m = mod_ref[0]
    u = u_ref[0].astype(F32)
    tm = u.shape[0]
    row = lax.broadcasted_iota(jnp.int32, (tm, 1), 0)
    prev_row = jnp.where(j == 0, 0.0, up_ref[0, HALO - 1:HALO, :].astype(F32))
    next_row = jnp.where(j == nj - 1, 0.0, un_ref[0, 0:1, :].astype(F32))
    u_prev = jnp.where(row == 0, prev_row, pltpu.roll(u, 1, 0))
    u_next = jnp.where(row == tm - 1, next_row, pltpu.roll(u, tm - 1, 0))
    cw = cw_ref[...]
    y = cw[0:1, :] * u_prev + cw[1:2, :] * u + cw[2:3, :] * u_next
    z = (bg_ref[0].astype(F32) * y).astype(BF16)
    mix = jnp.dot(z, w_ref[...], preferred_element_type=F32)
    x_new = _deepnorm_ln(x_ref[0], mix, m[2:3, :], g_ref[...], b_ref[...])
    x_out[0] = x_new
    h = x_new * (1.0 + m[4:5, :]) + m[3:4, :]
    for c in range(SUBLANES):
        h_out[0, c] = h[:, c * LANES:(c + 1) * LANES]
    hb = h.astype(BF16)
    lg = jnp.dot(hb, wr_ref[...], preferred_element_type=F32)
    lane = lax.broadcasted_iota(jnp.int32, lg.shape, 1).astype(F32)
    neg = jnp.float32(-jnp.inf)
    lg = jnp.where(lane < N_EXPERTS, lg, neg)
    m1 = jnp.max(lg, axis=-1, keepdims=True)
    i1 = jnp.min(jnp.where(lg == m1, lane, float(LANES)), axis=-1, keepdims=True)
    lg2 = jnp.where(lane == i1, neg, lg)
    m2 = jnp.max(lg2, axis=-1, keepdims=True)
    i2 = jnp.min(jnp.where(lg2 == m2, lane, float(LANES)), axis=-1, keepdims=True)
    e2 = jnp.exp(m2 - m1)
    gate1 = 1.0 / (1.0 + e2)
    gate2 = e2 / (1.0 + e2)
    member = jnp.where((lane == i1) | (lane == i2), 1.0, 0.0)
    route_out[0] = jnp.where(lane == RT_G1, gate1, jnp.where(lane == RT_G2, gate2, 0.0))
    r = lax.broadcasted_iota(jnp.int32, (tm, tm), 0)
    c = lax.broadcasted_iota(jnp.int32, (tm, tm), 1)
    tri = jnp.where(c <= r, 1.0, 0.0).astype(BF16)
    incl = jnp.dot(tri, member.astype(BF16), preferred_element_type=F32)
    before = incl - member + carry_ref[0:1, :]
    rank1 = jnp.sum(jnp.where(lane == i1, before, 0.0), axis=-1, keepdims=True)
    rank2 = jnp.sum(jnp.where(lane == i2, before, 0.0), axis=-1, keepdims=True)
    rec = jnp.where(lane == 0, rank1, jnp.where(lane == 1, rank2,
                    jnp.where(lane == 2, i1, jnp.where(lane == 3, i2, 0.0))))
    plan_out[...] = rec.T[:PLAN_ROWS, :].astype(jnp.int32)
    carry_ref[0:1, :] = carry_ref[0:1, :] + incl[tm - 1:tm, :]
    cnt_out[...] = carry_ref[...]


def _conv_out(u, bg, cw, w, x, mod, g, b, wr, *, tm):
    bsz, s, d = x.shape
    per = tm // HALO
    nh = s // HALO
    nj = s // tm
    main = lambda bi, j: (bi, j, 0)
    return pl.pallas_call(
        _convout_kernel,
        grid=(bsz, s // tm),
        in_specs=[
            pl.BlockSpec((1, tm, d), main),
            pl.BlockSpec((1, HALO, d), lambda bi, j: (bi, jnp.maximum(j * per - 1, 0), 0)),
            pl.BlockSpec((1, HALO, d), lambda bi, j: (bi, jnp.minimum((j + 1) * per, nh - 1), 0)),
            pl.BlockSpec((1, tm, d), main),
            _const_spec(cw.shape),
            _const_spec(w.shape),
            pl.BlockSpec((1, tm, d), main),
            pl.BlockSpec((1, MOD_ROWS, d), lambda bi, j: (bi, 0, 0)),
            _const_spec((1, d)),
            _const_spec((1, d)),
            _const_spec(wr.shape),
        ],
        out_specs=[
            pl.BlockSpec((1, tm, d), main),
            pl.BlockSpec((1, SUBLANES, tm, d // SUBLANES), lambda bi, j: (bi, 0, j, 0)),
            pl.BlockSpec((1, tm, LANES), main),
            pl.BlockSpec((PLAN_ROWS, tm), lambda bi, j: (0, bi * nj + j)),
            pl.BlockSpec((8, LANES), lambda bi, j: (0, 0)),
        ],
        out_shape=[
            jax.ShapeDtypeStruct((bsz, s, d), F32),
            jax.ShapeDtypeStruct((bsz, SUBLANES, s, d // SUBLANES), F32),
            jax.ShapeDtypeStruct((bsz, s, LANES), F32),
            jax.ShapeDtypeStruct((PLAN_ROWS, bsz * s), jnp.int32),
            jax.ShapeDtypeStruct((8, LANES), F32),
        ],
        scratch_shapes=[pltpu.VMEM((8, LANES), F32)],
        compiler_params=_cparams(("arbitrary", "arbitrary")),
        name="conv_out_ln_router",
    )(u, u, u, bg, cw, w, x, mod, g, b, wr)


MOE_TILE = 512


def _row_copy(src_ref, src_row, dst_ref, dst_row, sem):
    return pltpu.make_async_copy(src_ref.at[src_row], dst_ref.at[dst_row], sem)


def _chunked_row(ref, row):
    return ref.at[:, row, :]


def _chunk_copies(chunked_ref, tiled_ref, row0, sem, *, to_tiled):
    rows = chunked_ref.shape[1]
    out = []
    for c in range(SUBLANES):
        hbm = tiled_ref.at[pl.ds(row0, rows), c, :]
        vmem = chunked_ref.at[c]
        out.append(pltpu.make_async_copy(vmem, hbm, sem) if to_tiled else pltpu.make_async_copy(hbm, vmem, sem))
    return out


DMA_UNROLL = 8


def _dispatch_kernel(d1_ref, d2_ref, padlo_ref, padhi_ref, na_ref, h_ref, xs_ref, zero_ref, sem, sem2, zsem, *, tile):
    i = pl.program_id(0)
    tm = h_ref.shape[2]
    hsrc = h_ref.at[0]
    n_tiles = xs_ref.shape[0] // tile

    @pl.when(i == 0)
    def _():
        zero_ref[...] = jnp.zeros_like(zero_ref)

        def zrow_start(r, carry):
            _row_copy(zero_ref, 0, xs_ref, r, zsem).start()
            return carry

        def zrow_wait(r, carry):
            _row_copy(zero_ref, 0, xs_ref, r, zsem).wait()
            return carry

        def ztile_start(k, carry):
            pltpu.make_async_copy(zero_ref, xs_ref.at[pl.ds(k * tile, tile)], zsem).start()
            return carry

        def ztile_wait(k, carry):
            pltpu.make_async_copy(zero_ref, xs_ref.at[pl.ds(k * tile, tile)], zsem).wait()
            return carry

        for e in range(N_EXPERTS):
            lax.fori_loop(padlo_ref[e], padhi_ref[e], zrow_start, 0)
            lax.fori_loop(padlo_ref[e], padhi_ref[e], zrow_wait, 0)
        lax.fori_loop(na_ref[0], n_tiles, ztile_start, 0)
        lax.fori_loop(na_ref[0], n_tiles, ztile_wait, 0)

    base = i * tm

    def start(r, carry):
        pltpu.make_async_copy(_chunked_row(hsrc, r), xs_ref.at[d1_ref[base + r]], sem).start(priority=0)
        pltpu.make_async_copy(_chunked_row(hsrc, r), xs_ref.at[d2_ref[base + r]], sem2).start(priority=1)
        return carry

    def wait(r, carry):
        pltpu.make_async_copy(_chunked_row(hsrc, r), xs_ref.at[d1_ref[base + r]], sem).wait()
        pltpu.make_async_copy(_chunked_row(hsrc, r), xs_ref.at[d2_ref[base + r]], sem2).wait()
        return carry

    lax.fori_loop(0, tm, start, 0, unroll=DMA_UNROLL)
    lax.fori_loop(0, tm, wait, 0, unroll=DMA_UNROLL)


def _moe_dispatch(h, dest1, dest2, pad_lo, pad_hi, n_active, *, n_rows, tm, tile):
    bsz, nch, seq, lanes = h.shape
    per = seq // tm
    smem = pl.BlockSpec(memory_space=pltpu.SMEM)
    dma = pltpu.SemaphoreType.DMA(())
    return pl.pallas_call(
        functools.partial(_dispatch_kernel, tile=tile),
        grid=(bsz * per,),
        in_specs=[smem, smem, smem, smem, smem,
                  pl.BlockSpec((1, nch, tm, lanes), lambda i: (i // per, 0, i % per, 0))],
        out_specs=pl.BlockSpec(memory_space=pl.ANY),
        out_shape=jax.ShapeDtypeStruct((n_rows, nch, lanes), F32),
        scratch_shapes=[pltpu.VMEM((tile, nch, lanes), F32), dma, dma, dma],
        compiler_params=_cparams(("arbitrary",)),
        name="moe_dispatch",
    )(dest1, dest2, pad_lo, pad_hi, n_active, h)


def _expert_kernel(te_ref, na_ref, xs_ref, w1_ref, w3_ref, w2_ref, ys_ref, xbuf, obuf, xsem, osem, *, fc):
    i = pl.program_id(0)
    n = pl.num_programs(0)
    na = na_ref[0]
    tm = xbuf.shape[2]
    slot = i % 2

    def fetch(k):
        return _chunk_copies(xbuf.at[k % 2], xs_ref, k * tm, xsem.at[k % 2], to_tiled=False)

    def writeback(k):
        return _chunk_copies(obuf.at[k % 2], ys_ref, k * tm, osem.at[k % 2], to_tiled=True)

    @pl.when(i == 0)
    def _():
        for cp in fetch(i):
            cp.start()

    @pl.when(i + 1 < na)
    def _():
        for cp in fetch(i + 1):
            cp.start()

    @pl.when(i >= 2)
    def _():
        for cp in writeback(i - 2):
            cp.wait()

    @pl.when(i < na)
    def _():
        for cp in fetch(i):
            cp.wait()
        xb = jnp.concatenate([xbuf[slot, c] for c in range(SUBLANES)], axis=-1).astype(BF16)
        acc = jnp.zeros(xb.shape, F32)
        for c0 in range(0, w1_ref.shape[2], fc):
            a = jnp.dot(xb, w1_ref[0, :, c0:c0 + fc], preferred_element_type=F32)
            bb = jnp.dot(xb, w3_ref[0, :, c0:c0 + fc], preferred_element_type=F32)
            gact = (_silu(a) * bb).astype(BF16)
            acc = acc + jnp.dot(gact, w2_ref[0, c0:c0 + fc, :], preferred_element_type=F32)
        for c in range(SUBLANES):
            obuf[slot, c] = acc[:, c * LANES:(c + 1) * LANES]

    @pl.when(i >= na)
    def _():
        obuf[slot] = jnp.zeros(obuf.shape[1:], obuf.dtype)

    for cp in writeback(i):
        cp.start()

    @pl.when(i == n - 1)
    def _():
        for cp in writeback(i):
            cp.wait()

        @pl.when(i >= 1)
        def _():
            for cp in writeback(i - 1):
                cp.wait()


def _moe_experts(xs, tile_expert, n_active, w1, w3, w2, *, tm, fc):
    n_rows, nch, lanes = xs.shape
    _, d, ff = w1.shape

    def wspec(shape):
        return pl.BlockSpec(shape, lambda i, te, na: (te[i], 0, 0))

    grid_spec = pltpu.PrefetchScalarGridSpec(
        num_scalar_prefetch=2,
        grid=(n_rows // tm,),
        in_specs=[pl.BlockSpec(memory_space=pl.ANY), wspec((1, d, ff)), wspec((1, d, ff)), wspec((1, ff, d))],
        out_specs=pl.BlockSpec(memory_space=pl.ANY),
        scratch_shapes=[pltpu.VMEM((2, nch, tm, lanes), F32), pltpu.VMEM((2, nch, tm, lanes), F32),
                        pltpu.SemaphoreType.DMA((2,)), pltpu.SemaphoreType.DMA((2,))],
    )
    return pl.pallas_call(
        functools.partial(_expert_kernel, fc=fc),
        grid_spec=grid_spec,
        out_shape=jax.ShapeDtypeStruct((n_rows, nch, lanes), F32),
        compiler_params=_cparams(("arbitrary",)),
        name="moe_experts",
    )(tile_expert, n_active, xs, w1, w3, w2)


def _combine_kernel(d1_ref, d2_ref, ys_ref, rt_ref, x_ref, mod_ref, g_ref, b_ref, o_ref, y1_ref, y2_ref, sem, sem2):
    i = pl.program_id(0)
    tm = x_ref.shape[0]
    base = i * tm

    def start(r, carry):
        pltpu.make_async_copy(ys_ref.at[d1_ref[base + r]], _chunked_row(y1_ref, r), sem).start(priority=0)
        pltpu.make_async_copy(ys_ref.at[d2_ref[base + r]], _chunked_row(y2_ref, r), sem2).start(priority=1)
        return carry

    def wait(r, carry):
        pltpu.make_async_copy(ys_ref.at[d1_ref[base + r]], _chunked_row(y1_ref, r), sem).wait()
        pltpu.make_async_copy(ys_ref.at[d2_ref[base + r]], _chunked_row(y2_ref, r), sem2).wait()
        return carry

    lax.fori_loop(0, tm, start, 0, unroll=DMA_UNROLL)
    lax.fori_loop(0, tm, wait, 0, unroll=DMA_UNROLL)
    rt = rt_ref[...]
    y1 = jnp.concatenate([y1_ref[c] for c in range(SUBLANES)], axis=-1)
    y2 = jnp.concatenate([y2_ref[c] for c in range(SUBLANES)], axis=-1)
    y = rt[:, RT_G1:RT_G1 + 1] * y1 + rt[:, RT_G2:RT_G2 + 1] * y2
    o_ref[...] = _deepnorm_ln(x_ref[...], y, mod_ref[0][5:6, :], g_ref[...], b_ref[...])


def _moe_combine_ln(ys, dest1, dest2, rt, x, mod, g, b, *, rows_per_mod, tm):
    t, d = x.shape
    spm = rows_per_mod // tm
    smem = pl.BlockSpec(memory_space=pltpu.SMEM)
    dma = pltpu.SemaphoreType.DMA(())
    return pl.pallas_call(
        _combine_kernel,
        grid=(t // tm,),
        in_specs=[
            smem, smem,
            pl.BlockSpec(memory_space=pl.ANY),
            pl.BlockSpec((tm, LANES), lambda i: (i, 0)),
            pl.BlockSpec((tm, d), lambda i: (i, 0)),
            pl.BlockSpec((1, MOD_ROWS, d), lambda i: (i // spm, 0, 0)),
            _const_spec((1, d)),
            _const_spec((1, d)),
        ],
        out_specs=pl.BlockSpec((tm, d), lambda i: (i, 0)),
        out_shape=jax.ShapeDtypeStruct((t, d), F32),
        scratch_shapes=[pltpu.VMEM((SUBLANES, tm, LANES), F32), pltpu.VMEM((SUBLANES, tm, LANES), F32), dma, dma],
        compiler_params=_cparams(("arbitrary",)),
        name="moe_combine_ln",
    )(dest1, dest2, ys, rt, x, mod, g, b)


def _moe_layout(plan, counts, *, tile):
    n_pairs = plan.shape[1] * 2
    n_tiles = n_pairs // tile + N_EXPERTS
    cnt = counts[0, :N_EXPERTS].astype(jnp.int32)
    padded = (cnt + tile - 1) // tile * tile
    ends = jnp.cumsum(padded)
    starts = ends - padded
    dest1 = starts[plan[2]] + plan[0]
    dest2 = starts[plan[3]] + plan[1]
    n_active = ends[-1:] // tile
    tiles = jnp.minimum(jnp.arange(n_tiles, dtype=jnp.int32), n_active[0] - 1) * tile
    tile_expert = jnp.sum((ends[None, :] <= tiles[:, None]).astype(jnp.int32), axis=1)
    tile_expert = jnp.minimum(tile_expert, N_EXPERTS - 1)
    return dest1, dest2, starts + cnt, ends, tile_expert, n_active.astype(jnp.int32), n_tiles * tile


def _rope_tables(rows_count):
    half = HEAD_DIM // 2
    rows = jnp.repeat(jnp.arange(rows_count, dtype=F32), GRID_W)
    cols = jnp.tile(jnp.arange(GRID_W, dtype=F32), rows_count)
    inv_freq = ROPE_THETA ** (-jnp.arange(0, half, 2, dtype=F32) / half)
    ar = rows[:, None] * inv_freq
    ac = cols[:, None] * inv_freq
    ang = jnp.concatenate([ar, ar, ac, ac], axis=-1)
    sign = jnp.where((jnp.arange(HEAD_DIM) % 32) < 16, -1.0, 1.0).astype(F32)
    cos = jnp.tile(jnp.cos(ang), (1, LANES // HEAD_DIM))
    sin = jnp.tile(jnp.sin(ang) * sign, (1, LANES // HEAD_DIM))
    return cos, sin


def _dup_heads(w, n_heads):
    d = w.shape[0]
    w = w.reshape(d, n_heads, 1, HEAD_DIM)
    return jnp.broadcast_to(w, (d, n_heads, 2, HEAD_DIM)).reshape(d, n_heads * 2 * HEAD_DIM)


def _mod_blocks(mod_rows, d):
    r = mod_rows.shape[0]
    m = mod_rows.reshape(r, 6, d)
    return jnp.concatenate([m, jnp.zeros((r, MOD_ROWS - 6, d), F32)], axis=1)


def kernel(x, c, ctx, c_ctx, w_mod, b_mod, ln_g, ln_b, attn_w_qkv, attn_w_o, attn_sink,
           conv_w_in, conv_w, conv_w_out, ffn_w1, ffn_w3, ffn_w2,
           moe_router, moe_w1, moe_w3, moe_w2):
    bsz, seq, d = x.shape
    n_ctx = ctx.shape[1]
    t = bsz * seq
    q_dim = attn_w_o.shape[1]
    kv_dim = (attn_w_qkv.shape[2] - q_dim) // 2
    n_kv = kv_dim // HEAD_DIM

    pad_rows = (-(bsz + 1)) % 8
    cvec = jnp.concatenate([c, c_ctx[None, :], jnp.zeros((pad_rows, d), F32)], axis=0)
    mod_all = _modulation(cvec, w_mod, b_mod)
    mod0 = _mod_blocks(mod_all[0, :bsz], d)
    modc0 = _mod_blocks(mod_all[0, bsz:bsz + 1], d)
    mod1 = _mod_blocks(mod_all[1, :bsz], d)

    cos, sin = _rope_tables(seq // GRID_W)

    wqkv = attn_w_qkv[0]
    wk = _dup_heads(wqkv[:, q_dim:q_dim + kv_dim], n_kv)
    wv = _dup_heads(wqkv[:, q_dim + kv_dim:], n_kv)
    w_all = jnp.concatenate([wqkv[:, :q_dim], wk, wv], axis=1).astype(BF16)
    w_kv = jnp.concatenate([wk, wv], axis=1).astype(BF16)
    x2d = x.reshape(t, d)
    q, kd, vd = _project(x2d, mod0, w_all, cos, sin, rows_per_mod=seq, tm=512,
                         rope_lanes=q_dim + 2 * kv_dim, q_lanes=q_dim,
                         splits=(q_dim, 2 * kv_dim, 2 * kv_dim), name="qkv_rope")
    (kvc,) = _project(ctx.reshape(bsz * n_ctx, d), modc0, w_kv, cos, sin, rows_per_mod=bsz * n_ctx, tm=512,
                      rope_lanes=0, q_lanes=0, splits=(4 * kv_dim,), name="ctx_kv")
    attn, w1b, w3b, w2b = _attention(q.reshape(bsz, seq, q_dim), kd.reshape(bsz, seq, 2 * kv_dim),
                                     vd.reshape(bsz, seq, 2 * kv_dim), kvc.reshape(bsz, n_ctx, 4 * kv_dim),
                                     attn_sink[0], moe_w1[0], moe_w3[0], moe_w2[0], tq=256)
    ln0 = jnp.concatenate([ln_g[0, 0][None], ln_b[0, 0][None], ln_g[0, 1][None], ln_b[0, 1][None],
                           jnp.zeros((4, d), F32)], axis=0)
    x2, u, bg = _layer0_tail(attn.reshape(t, q_dim), x2d, mod0, mod1, attn_w_o[0].astype(BF16),
                             ffn_w1[0].astype(BF16), ffn_w3[0].astype(BF16), ffn_w2[0].astype(BF16),
                             conv_w_in[0].astype(BF16), ln0, rows_per_mod=seq, tm=512, fc=256)

    cw = jnp.concatenate([conv_w[0], jnp.zeros((8 - CONV_WIDTH, d), F32)], axis=0)
    wr = jnp.concatenate([moe_router[0], jnp.zeros((d, LANES - N_EXPERTS), F32)], axis=1).astype(BF16)
    x3, h4, rt, plan, counts = _conv_out(u.reshape(bsz, seq, d), bg.reshape(bsz, seq, d), cw,
                                         conv_w_out[0].astype(BF16), x2.reshape(bsz, seq, d), mod1,
                                         ln_g[1, 0][None], ln_b[1, 0][None], wr, tm=512)
    rt = rt.reshape(t, LANES)
    dest1, dest2, pad_lo, pad_hi, tile_expert, n_active, n_rows = _moe_layout(plan, counts, tile=MOE_TILE)
    xs = _moe_dispatch(h4, dest1, dest2, pad_lo, pad_hi, n_active, n_rows=n_rows, tm=512, tile=MOE_TILE)
    ys = _moe_experts(xs, tile_expert, n_active, w1b, w3b, w2b, tm=MOE_TILE, fc=512)
    out = _moe_combine_ln(ys, dest1, dest2, rt, x3.reshape(t, d), mod1, ln_g[1, 1][None], ln_b[1, 1][None],
                          rows_per_mod=seq, tm=512)
    return out.reshape(bsz, seq, d)
```

```python
import functools

import jax
import jax.numpy as jnp
from jax import lax
from jax.experimental import pallas as pl
from jax.experimental.pallas import tpu as pltpu

F32 = jnp.float32
BF16 = jnp.bfloat16

GRID_W = 64
HEAD_DIM = 64
GQA_GROUP = 4
WINDOW = 128
ROPE_THETA = 10000.0
CONV_WIDTH = 3
N_EXPERTS = 8
LN_EPS = 1e-5
MASK_VALUE = -1e30
DEPTH = 2
DEEPNORM_ALPHA = (2 * DEPTH) ** 0.25
LOG2_E = 1.4426950408889634

LANES = 128
MOD_ROWS = 8
VMEM_LIMIT = 56 * 1024 * 1024


def _cparams(sem):
    return pltpu.CompilerParams(dimension_semantics=sem, vmem_limit_bytes=VMEM_LIMIT)


def _const_spec(shape):
    nd = len(shape)
    return pl.BlockSpec(shape, lambda *_: (0,) * nd, pipeline_mode=pl.Buffered(1))


def _silu(a):
    return a / (1.0 + jnp.exp(-a))


SUBLANES = 8


def _deepnorm_ln(x, mix, gate, g, b):
    y = DEEPNORM_ALPHA * x + gate * mix
    mu = jnp.mean(y, axis=-1, keepdims=True)
    d = y - mu
    var = jnp.mean(d * d, axis=-1, keepdims=True)
    return d * lax.rsqrt(var + LN_EPS) * g + b


def _mod_kernel(c_ref, w_ref, b_ref, o_ref):
    cv = c_ref[...]
    s = _silu(cv).astype(BF16)
    o_ref[0] = jnp.dot(s, w_ref[0].astype(BF16), preferred_element_type=F32) + b_ref[0]


def _modulation(cvec, w_mod, b_mod):
    depth, d, n = w_mod.shape
    rows = cvec.shape[0]
    nt = 1536
    return pl.pallas_call(
        _mod_kernel,
        grid=(depth, n // nt),
        in_specs=[
            pl.BlockSpec((rows, d), lambda l, j: (0, 0)),
            pl.BlockSpec((1, d, nt), lambda l, j: (l, 0, j)),
            pl.BlockSpec((1, 1, nt), lambda l, j: (l, 0, j)),
        ],
        out_specs=pl.BlockSpec((1, rows, nt), lambda l, j: (l, 0, j)),
        out_shape=jax.ShapeDtypeStruct((depth, rows, n), F32),
        compiler_params=_cparams(("parallel", "parallel")),
        name="modulation",
    )(cvec, w_mod, b_mod.reshape(depth, 1, n))


def _proj_kernel(x_ref, mod_ref, w_ref, cos_ref, sin_ref, *o_refs, rope_lanes, q_lanes, splits):
    m = mod_ref[0]
    h = x_ref[...] * (1.0 + m[1:2, :]) + m[0:1, :]
    y = jnp.dot(h.astype(BF16), w_ref[...], preferred_element_type=F32)
    if rope_lanes:
        cos = cos_ref[...]
        sin = sin_ref[...]
        first_half = (lax.broadcasted_iota(jnp.int32, (1, LANES), 1) % 32) < 16
    off = 0
    for o_ref, width in zip(o_refs, splits):
        for c0 in range(0, width, LANES):
            yc = y[:, off + c0:off + c0 + LANES]
            if off + c0 < rope_lanes:
                rot = jnp.where(first_half, pltpu.roll(yc, LANES - 16, 1), pltpu.roll(yc, 16, 1))
                yc = yc * cos + rot * sin
                if off + c0 < q_lanes:
                    yc = yc * (HEAD_DIM ** -0.5 * LOG2_E)
            o_ref[:, c0:c0 + LANES] = yc.astype(o_ref.dtype)
        off += width


def _project(x, mod, w, cos, sin, *, rows_per_mod, tm, rope_lanes, q_lanes, splits, name):
    t, d = x.shape
    n = w.shape[1]
    steps_per_mod = rows_per_mod // tm
    steps_per_seq = cos.shape[0] // tm
    kern = functools.partial(_proj_kernel, rope_lanes=rope_lanes, q_lanes=q_lanes, splits=splits)
    return pl.pallas_call(
        kern,
        grid=(t // tm,),
        in_specs=[
            pl.BlockSpec((tm, d), lambda i: (i, 0)),
            pl.BlockSpec((1, MOD_ROWS, d), lambda i: (i // steps_per_mod, 0, 0)),
            _const_spec((d, n)),
            pl.BlockSpec((tm, LANES), lambda i: (i % steps_per_seq, 0)),
            pl.BlockSpec((tm, LANES), lambda i: (i % steps_per_seq, 0)),
        ],
        out_specs=[pl.BlockSpec((tm, s), lambda i: (i, 0)) for s in splits],
        out_shape=[jax.ShapeDtypeStruct((t, s), BF16) for s in splits],
        compiler_params=_cparams(("parallel",)),
        name=name,
    )(x, mod, w, cos, sin)


def _attn_kernel(sink_ref, q_ref, km_ref, kp_ref, kn_ref, vm_ref, vp_ref, vn_ref, kvc_ref, w1_ref, w3_ref, w2_ref,
                 o_ref, w1_out, w3_out, w2_out, *, tq, seq, n_ctx):
    w1_out[...] = w1_ref[...].astype(w1_out.dtype)
    w3_out[...] = w3_ref[...].astype(w3_out.dtype)
    w2_out[...] = w2_ref[...].astype(w2_out.dtype)
    i = pl.program_id(1)
    nsub = tq // WINDOW
    span = 3 * WINDOW
    nkv = km_ref.shape[2] // LANES
    lane = lax.broadcasted_iota(jnp.int32, (1, LANES), 1)
    lo = lane < HEAD_DIM
    r = lax.broadcasted_iota(jnp.int32, (WINDOW, WINDOW), 0)
    c = lax.broadcasted_iota(jnp.int32, (WINDOW, WINDOW), 1)
    valid_first, valid_last = [], []
    for j in range(nsub):
        blk = i * (tq // WINDOW) + j
        valid_first.append((c >= r) & (blk > 0))
        valid_last.append((c <= r) & (blk < seq // WINDOW - 1))
    gidx = lax.broadcasted_iota(jnp.int32, (GQA_GROUP, 1, 1), 0)
    zero = jnp.zeros((), BF16)
    for kh in range(nkv):
        ks = slice(kh * LANES, (kh + 1) * LANES)
        kall = jnp.concatenate([kp_ref[0, :, ks], km_ref[0, :, ks], kn_ref[0, :, ks]], axis=0)
        vall = jnp.concatenate([vp_ref[0, :, ks], vm_ref[0, :, ks], vn_ref[0, :, ks]], axis=0)
        kc = kvc_ref[0, :, ks]
        vc = kvc_ref[0, :, nkv * LANES + kh * LANES:nkv * LANES + (kh + 1) * LANES]
        sink = jnp.full((GQA_GROUP, 1, 1), sink_ref[kh * GQA_GROUP], F32)
        for g in range(1, GQA_GROUP):
            sink = jnp.where(gidx == g, sink_ref[kh * GQA_GROUP + g], sink)
        sink = sink * LOG2_E
        for j in range(nsub):
            rows = slice(j * WINDOW, (j + 1) * WINDOW)
            kk = jnp.concatenate([kc, kall[j * WINDOW:j * WINDOW + span]], axis=0)
            vv = jnp.concatenate([vc, vall[j * WINDOW:j * WINDOW + span]], axis=0)
            parts = []
            for t in range(GQA_GROUP // 2):
                q2 = q_ref[0, rows, (2 * kh + t) * LANES:(2 * kh + t + 1) * LANES]
                parts += [jnp.where(lo, q2, zero), jnp.where(lo, zero, q2)]
            lhs = jnp.concatenate(parts, axis=0)
            s = lax.dot_general(lhs, kk, (((1,), (1,)), ((), ())), preferred_element_type=F32)
            s = s.reshape(GQA_GROUP, WINDOW, n_ctx + span)
            s = jnp.concatenate([
                s[:, :, :n_ctx],
                jnp.where(valid_first[j][None], s[:, :, n_ctx:n_ctx + WINDOW], MASK_VALUE),
                s[:, :, n_ctx + WINDOW:n_ctx + 2 * WINDOW],
                jnp.where(valid_last[j][None], s[:, :, n_ctx + 2 * WINDOW:], MASK_VALUE),
            ], axis=-1)
            m = jnp.maximum(jnp.max(s, axis=-1, keepdims=True), sink)
            p = jnp.exp2(s - m)
            denom = jnp.sum(p, axis=-1, keepdims=True) + jnp.exp2(sink - m)
            o = jnp.dot(p.astype(BF16).reshape(GQA_GROUP * WINDOW, n_ctx + span), vv, preferred_element_type=F32)
            o = o.reshape(GQA_GROUP, WINDOW, LANES) / denom
            for t in range(GQA_GROUP // 2):
                o2 = jnp.where(lo, o[2 * t], o[2 * t + 1])
                o_ref[0, rows, (2 * kh + t) * LANES:(2 * kh + t + 1) * LANES] = o2.astype(o_ref.dtype)


def _attention(q, kd, vd, kvc, sink, w1, w3, w2, *, tq):
    b, s, qd = q.shape
    kvd = kd.shape[2]
    n_ctx = kvc.shape[1]
    nblk = s // WINDOW
    per = tq // WINDOW
    nq = s // tq
    steps = b * nq
    ne, d, ff = w1.shape
    w13 = (ne * d // steps, ff)
    w2s = (ne * ff // steps, d)
    kern = functools.partial(_attn_kernel, tq=tq, seq=s, n_ctx=n_ctx)
    main = lambda bi, i: (bi, i, 0)
    prev = lambda bi, i: (bi, jnp.maximum(i * per - 1, 0), 0)
    nxt = lambda bi, i: (bi, jnp.minimum((i + 1) * per, nblk - 1), 0)
    wrows = lambda bi, i: (bi * nq + i, 0)
    out, w1b, w3b, w2b = pl.pallas_call(
        kern,
        grid=(b, nq),
        in_specs=[
            pl.BlockSpec(memory_space=pltpu.SMEM),
            pl.BlockSpec((1, tq, qd), main),
            pl.BlockSpec((1, tq, kvd), main),
            pl.BlockSpec((1, WINDOW, kvd), prev),
            pl.BlockSpec((1, WINDOW, kvd), nxt),
            pl.BlockSpec((1, tq, kvd), main),
            pl.BlockSpec((1, WINDOW, kvd), prev),
            pl.BlockSpec((1, WINDOW, kvd), nxt),
            pl.BlockSpec((1, n_ctx, 2 * kvd), lambda bi, i: (bi, 0, 0)),
            pl.BlockSpec(w13, wrows), pl.BlockSpec(w13, wrows), pl.BlockSpec(w2s, wrows),
        ],
        out_specs=[pl.BlockSpec((1, tq, qd), main),
                   pl.BlockSpec(w13, wrows), pl.BlockSpec(w13, wrows), pl.BlockSpec(w2s, wrows)],
        out_shape=[jax.ShapeDtypeStruct((b, s, qd), BF16),
                   jax.ShapeDtypeStruct((ne * d, ff), BF16), jax.ShapeDtypeStruct((ne * d, ff), BF16),
                   jax.ShapeDtypeStruct((ne * ff, d), BF16)],
        compiler_params=_cparams(("parallel", "parallel")),
        name="window_attention",
    )(sink, q, kd, kd, kd, vd, vd, vd, kvc, w1.reshape(ne * d, ff), w3.reshape(ne * d, ff),
      w2.reshape(ne * ff, d))
    return out, w1b.reshape(ne, d, ff), w3b.reshape(ne, d, ff), w2b.reshape(ne, ff, d)


def _tail0_kernel(a_ref, x_ref, mod0_ref, mod1_ref, wo_ref, w1_ref, w3_ref, w2_ref, win_ref, ln_ref,
                  x_out, u_out, bg_out, *, fc):
    m0 = mod0_ref[0]
    m1 = mod1_ref[0]
    ln = ln_ref[...]
    d = x_ref.shape[1]
    mix = jnp.dot(a_ref[...], wo_ref[...], preferred_element_type=F32)
    x1 = _deepnorm_ln(x_ref[...], mix, m0[2:3, :], ln[0:1, :], ln[1:2, :])
    hb = (x1 * (1.0 + m0[4:5, :]) + m0[3:4, :]).astype(BF16)
    acc = jnp.zeros(x1.shape, F32)
    for c0 in range(0, w1_ref.shape[1], fc):
        a = jnp.dot(hb, w1_ref[:, c0:c0 + fc], preferred_element_type=F32)
        bb = jnp.dot(hb, w3_ref[:, c0:c0 + fc], preferred_element_type=F32)
        gact = (_silu(a) * bb).astype(BF16)
        acc = acc + jnp.dot(gact, w2_ref[c0:c0 + fc, :], preferred_element_type=F32)
    x2 = _deepnorm_ln(x1, acc, m0[5:6, :], ln[2:3, :], ln[3:4, :])
    x_out[...] = x2
    hb = (x2 * (1.0 + m1[1:2, :]) + m1[0:1, :]).astype(BF16)
    bg_out[...] = jnp.dot(hb, win_ref[:, :d], preferred_element_type=F32).astype(bg_out.dtype)
    cg = jnp.dot(hb, win_ref[:, d:2 * d], preferred_element_type=F32)
    val = jnp.dot(hb, win_ref[:, 2 * d:], preferred_element_type=F32)
    u_out[...] = (cg * val).astype(u_out.dtype)


def _layer0_tail(a, x, mod0, mod1, wo, w1, w3, w2, win, ln, *, rows_per_mod, tm, fc):
    t, d = x.shape
    spm = rows_per_mod // tm
    row = lambda i: (i, 0)
    modi = lambda i: (i // spm, 0, 0)
    return pl.pallas_call(
        functools.partial(_tail0_kernel, fc=fc),
        grid=(t // tm,),
        in_specs=[
            pl.BlockSpec((tm, a.shape[1]), row),
            pl.BlockSpec((tm, d), row),
            pl.BlockSpec((1, MOD_ROWS, d), modi),
            pl.BlockSpec((1, MOD_ROWS, d), modi),
            _const_spec(wo.shape),
            _const_spec(w1.shape),
            _const_spec(w3.shape),
            _const_spec(w2.shape),
            _const_spec(win.shape),
            _const_spec(ln.shape),
        ],
        out_specs=[pl.BlockSpec((tm, d), row)] * 3,
        out_shape=[jax.ShapeDtypeStruct((t, d), F32), jax.ShapeDtypeStruct((t, d), BF16),
                   jax.ShapeDtypeStruct((t, d), BF16)],
        compiler_params=_cparams(("parallel",)),
        name="oproj_ffn_convin",
    )(a, x, mod0, mod1, wo, w1, w3, w2, win, ln)


HALO = 16
RT_G1, RT_G2 = 8, 9
PLAN_ROWS = 8


def _convout_kernel(u_ref, up_ref, un_ref, bg_ref, cw_ref, w_ref, x_ref, mod_ref, g_ref, b_ref, wr_ref,
                    x_out, h_out, route_out, plan_out, cnt_out, carry_ref):
    j = pl.program_id(1)
    nj = pl.num_programs(1)

    @pl.when((pl.program_id(0) == 0) & (j == 0))
    def _():
        carry_ref[...] = jnp.zeros_like(carry_ref)

    m = mod_ref[0]
    u = u_ref[0].astype(F32)
    tm = u.shape[0]
    row = lax.broadcasted_iota(jnp.int32, (tm, 1), 0)
    prev_row = jnp.where(j == 0, 0.0, up_ref[0, HALO - 1:HALO, :].astype(F32))
    next_row = jnp.where(j == nj - 1, 0.0, un_ref[0, 0:1, :].astype(F32))
    u_prev = jnp.where(row == 0, prev_row, pltpu.roll(u, 1, 0))
    u_next = jnp.where(row == tm - 1, next_row, pltpu.roll(u, tm - 1, 0))
    cw = cw_ref[...]
    y = cw[0:1, :] * u_prev + cw[1:2, :] * u + cw[2:3, :] * u_next
    z = (bg_ref[0].astype(F32) * y).astype(BF16)
    mix = jnp.dot(z, w_ref[...], preferred_element_type=F32)
    x_new = _deepnorm_ln(x_ref[0], mix, m[2:3, :], g_ref[...], b_ref[...])
    x_out[0] = x_new
    h = x_new * (1.0 + m[4:5, :]) + m[3:4, :]
    for c in range(SUBLANES):
        h_out[0, c] = h[:, c * LANES:(c + 1) * LANES]
    hb = h.astype(BF16)
    lg = jnp.dot(hb, wr_ref[...], preferred_element_type=F32)
    lane = lax.broadcasted_iota(jnp.int32, lg.shape, 1).astype(F32)
    neg = jnp.float32(-jnp.inf)
    lg = jnp.where(lane < N_EXPERTS, lg, neg)
    m1 = jnp.max(lg, axis=-1, keepdims=True)
    i1 = jnp.min(jnp.where(lg == m1, lane, float(LANES)), axis=-1, keepdims=True)
    lg2 = jnp.where(lane == i1, neg, lg)
    m2 = jnp.max(lg2, axis=-1, keepdims=True)
    i2 = jnp.min(jnp.where(lg2 == m2, lane, float(LANES)), axis=-1, keepdims=True)
    e2 = jnp.exp(m2 - m1)
    gate1 = 1.0 / (1.0 + e2)
    gate2 = e2 / (1.0 + e2)
    member = jnp.where((lane == i1) | (lane == i2), 1.0, 0.0)
    route_out[0] = jnp.where(lane == RT_G1, gate1, jnp.where(lane == RT_G2, gate2, 0.0))
    r = lax.broadcasted_iota(jnp.int32, (tm, tm), 0)
    c = lax.broadcasted_iota(jnp.int32, (tm, tm), 1)
    tri = jnp.where(c <= r, 1.0, 0.0).astype(BF16)
    incl = jnp.dot(tri, member.astype(BF16), preferred_element_type=F32)
    before = incl - member + carry_ref[0:1, :]
    rank1 = jnp.sum(jnp.where(lane == i1, before, 0.0), axis=-1, keepdims=True)
    rank2 = jnp.sum(jnp.where(lane == i2, before, 0.0), axis=-1, keepdims=True)
    rec = jnp.where(lane == 0, rank1, jnp.where(lane == 1, rank2,
                    jnp.where(lane == 2, i1, jnp.where(lane == 3, i2, 0.0))))
    plan_out[...] = rec.T[:PLAN_ROWS, :].astype(jnp.int32)
    carry_ref[0:1, :] = carry_ref[0:1, :] + incl[tm - 1:tm, :]
    cnt_out[...] = carry_ref[...]


def _conv_out(u, bg, cw, w, x, mod, g, b, wr, *, tm):
    bsz, s, d = x.shape
    per = tm // HALO
    nh = s // HALO
    nj = s // tm
    main = lambda bi, j: (bi, j, 0)
    return pl.pallas_call(
        _convout_kernel,
        grid=(bsz, s // tm),
        in_specs=[
            pl.BlockSpec((1, tm, d), main),
            pl.BlockSpec((1, HALO, d), lambda bi, j: (bi, jnp.maximum(j * per - 1, 0), 0)),
            pl.BlockSpec((1, HALO, d), lambda bi, j: (bi, jnp.minimum((j + 1) * per, nh - 1), 0)),
            pl.BlockSpec((1, tm, d), main),
            _const_spec(cw.shape),
            _const_spec(w.shape),
            pl.BlockSpec((1, tm, d), main),
            pl.BlockSpec((1, MOD_ROWS, d), lambda bi, j: (bi, 0, 0)),
            _const_spec((1, d)),
            _const_spec((1, d)),
            _const_spec(wr.shape),
        ],
        out_specs=[
            pl.BlockSpec((1, tm, d), main),
            pl.BlockSpec((1, SUBLANES, tm, d // SUBLANES), lambda bi, j: (bi, 0, j, 0)),
            pl.BlockSpec((1, tm, LANES), main),
            pl.BlockSpec((PLAN_ROWS, tm), lambda bi, j: (0, bi * nj + j)),
            pl.BlockSpec((8, LANES), lambda bi, j: (0, 0)),
        ],
        out_shape=[
            jax.ShapeDtypeStruct((bsz, s, d), F32),
            jax.ShapeDtypeStruct((bsz, SUBLANES, s, d // SUBLANES), F32),
            jax.ShapeDtypeStruct((bsz, s, LANES), F32),
            jax.ShapeDtypeStruct((PLAN_ROWS, bsz * s), jnp.int32),
            jax.ShapeDtypeStruct((8, LANES), F32),
        ],
        scratch_shapes=[pltpu.VMEM((8, LANES), F32)],
        compiler_params=_cparams(("arbitrary", "arbitrary")),
        name="conv_out_ln_router",
    )(u, u, u, bg, cw, w, x, mod, g, b, wr)


MOE_TILE = 512


def _row_copy(src_ref, src_row, dst_ref, dst_row, sem):
    return pltpu.make_async_copy(src_ref.at[src_row], dst_ref.at[dst_row], sem)


def _chunked_row(ref, row):
    return ref.at[:, row, :]


def _chunk_copies(chunked_ref, tiled_ref, row0, sem, *, to_tiled):
    rows = chunked_ref.shape[1]
    out = []
    for c in range(SUBLANES):
        hbm = tiled_ref.at[pl.ds(row0, rows), c, :]
        vmem = chunked_ref.at[c]
        out.append(pltpu.make_async_copy(vmem, hbm, sem) if to_tiled else pltpu.make_async_copy(hbm, vmem, sem))
    return out


DMA_UNROLL = 8


def _dispatch_kernel(d1_ref, d2_ref, padlo_ref, padhi_ref, na_ref, h_ref, xs_ref, zero_ref, sem, sem2, zsem, *, tile):
    i = pl.program_id(0)
    tm = h_ref.shape[2]
    hsrc = h_ref.at[0]
    n_tiles = xs_ref.shape[0] // tile

    @pl.when(i == 0)
    def _():
        zero_ref[...] = jnp.zeros_like(zero_ref)

        def zrow_start(r, carry):
            _row_copy(zero_ref, 0, xs_ref, r, zsem).start()
            return carry

        def zrow_wait(r, carry):
            _row_copy(zero_ref, 0, xs_ref, r, zsem).wait()
            return carry

        def ztile_start(k, carry):
            pltpu.make_async_copy(zero_ref, xs_ref.at[pl.ds(k * tile, tile)], zsem).start()
            return carry

        def ztile_wait(k, carry):
            pltpu.make_async_copy(zero_ref, xs_ref.at[pl.ds(k * tile, tile)], zsem).wait()
            return carry

        for e in range(N_EXPERTS):
            lax.fori_loop(padlo_ref[e], padhi_ref[e], zrow_start, 0)
            lax.fori_loop(padlo_ref[e], padhi_ref[e], zrow_wait, 0)
        lax.fori_loop(na_ref[0], n_tiles, ztile_start, 0)
        lax.fori_loop(na_ref[0], n_tiles, ztile_wait, 0)

    base = i * tm

    def start(r, carry):
        pltpu.make_async_copy(_chunked_row(hsrc, r), xs_ref.at[d1_ref[base + r]], sem).start(priority=0)
        pltpu.make_async_copy(_chunked_row(hsrc, r), xs_ref.at[d2_ref[base + r]], sem2).start(priority=1)
        return carry

    def wait(r, carry):
        pltpu.make_async_copy(_chunked_row(hsrc, r), xs_ref.at[d1_ref[base + r]], sem).wait()
        pltpu.make_async_copy(_chunked_row(hsrc, r), xs_ref.at[d2_ref[base + r]], sem2).wait()
        return carry

    lax.fori_loop(0, tm, start, 0, unroll=DMA_UNROLL)
    lax.fori_loop(0, tm, wait, 0, unroll=DMA_UNROLL)


def _moe_dispatch(h, dest1, dest2, pad_lo, pad_hi, n_active, *, n_rows, tm, tile):
    bsz, nch, seq, lanes = h.shape
    per = seq // tm
    smem = pl.BlockSpec(memory_space=pltpu.SMEM)
    dma = pltpu.SemaphoreType.DMA(())
    return pl.pallas_call(
        functools.partial(_dispatch_kernel, tile=tile),
        grid=(bsz * per,),
        in_specs=[smem, smem, smem, smem, smem,
                  pl.BlockSpec((1, nch, tm, lanes), lambda i: (i // per, 0, i % per, 0))],
        out_specs=pl.BlockSpec(memory_space=pl.ANY),
        out_shape=jax.ShapeDtypeStruct((n_rows, nch, lanes), F32),
        scratch_shapes=[pltpu.VMEM((tile, nch, lanes), F32), dma, dma, dma],
        compiler_params=_cparams(("arbitrary",)),
        name="moe_dispatch",
    )(dest1, dest2, pad_lo, pad_hi, n_active, h)


def _expert_kernel(te_ref, na_ref, xs_ref, w1_ref, w3_ref, w2_ref, ys_ref, xbuf, obuf, xsem, osem, *, fc):
    i = pl.program_id(0)
    n = pl.num_programs(0)
    na = na_ref[0]
    tm = xbuf.shape[2]
    slot = i % 2

    def fetch(k):
        return _chunk_copies(xbuf.at[k % 2], xs_ref, k * tm, xsem.at[k % 2], to_tiled=False)

    def writeback(k):
        return _chunk_copies(obuf.at[k % 2], ys_ref, k * tm, osem.at[k % 2], to_tiled=True)

    @pl.when(i == 0)
    def _():
        for cp in fetch(i):
            cp.start()

    @pl.when(i + 1 < na)
    def _():
        for cp in fetch(i + 1):
            cp.start()

    @pl.when(i >= 2)
    def _():
        for cp in writeback(i - 2):
            cp.wait()

    @pl.when(i < na)
    def _():
        for cp in fetch(i):
            cp.wait()
        xb = jnp.concatenate([xbuf[slot, c] for c in range(SUBLANES)], axis=-1).astype(BF16)
        acc = jnp.zeros(xb.shape, F32)
        for c0 in range(0, w1_ref.shape[2], fc):
            a = jnp.dot(xb, w1_ref[0, :, c0:c0 + fc], preferred_element_type=F32)
            bb = jnp.dot(xb, w3_ref[0, :, c0:c0 + fc], preferred_element_type=F32)
            gact = (_silu(a) * bb).astype(BF16)
            acc = acc + jnp.dot(gact, w2_ref[0, c0:c0 + fc, :], preferred_element_type=F32)
        for c in range(SUBLANES):
            obuf[slot, c] = acc[:, c * LANES:(c + 1) * LANES]

    @pl.when(i >= na)
    def _():
        obuf[slot] = jnp.zeros(obuf.shape[1:], obuf.dtype)

    for cp in writeback(i):
        cp.start()

    @pl.when(i == n - 1)
    def _():
        for cp in writeback(i):
            cp.wait()

        @pl.when(i >= 1)
        def _():
            for cp in writeback(i - 1):
                cp.wait()


def _moe_experts(xs, tile_expert, n_active, w1, w3, w2, *, tm, fc):
    n_rows, nch, lanes = xs.shape
    _, d, ff = w1.shape

    def wspec(shape):
        return pl.BlockSpec(shape, lambda i, te, na: (te[i], 0, 0))

    grid_spec = pltpu.PrefetchScalarGridSpec(
        num_scalar_prefetch=2,
        grid=(n_rows // tm,),
        in_specs=[pl.BlockSpec(memory_space=pl.ANY), wspec((1, d, ff)), wspec((1, d, ff)), wspec((1, ff, d))],
        out_specs=pl.BlockSpec(memory_space=pl.ANY),
        scratch_shapes=[pltpu.VMEM((2, nch, tm, lanes), F32), pltpu.VMEM((2, nch, tm, lanes), F32),
                        pltpu.SemaphoreType.DMA((2,)), pltpu.SemaphoreType.DMA((2,))],
    )
    return pl.pallas_call(
        functools.partial(_expert_kernel, fc=fc),
        grid_spec=grid_spec,
        out_shape=jax.ShapeDtypeStruct((n_rows, nch, lanes), F32),
        compiler_params=_cparams(("arbitrary",)),
        name="moe_experts",
    )(tile_expert, n_active, xs, w1, w3, w2)


def _combine_kernel(d1_ref, d2_ref, ys_ref, rt_ref, x_ref, mod_ref, g_ref, b_ref, o_ref, y1_ref, y2_ref, sem, sem2):
    i = pl.program_id(0)
    tm = x_ref.shape[0]
    base = i * tm

    def start(r, carry):
        pltpu.make_async_copy(ys_ref.at[d1_ref[base + r]], _chunked_row(y1_ref, r), sem).start(priority=0)
        pltpu.make_async_copy(ys_ref.at[d2_ref[base + r]], _chunked_row(y2_ref, r), sem2).start(priority=1)
        return carry

    def wait(r, carry):
        pltpu.make_async_copy(ys_ref.at[d1_ref[base + r]], _chunked_row(y1_ref, r), sem).wait()
        pltpu.make_async_copy(ys_ref.at[d2_ref[base + r]], _chunked_row(y2_ref, r), sem2).wait()
        return carry

    lax.fori_loop(0, tm, start, 0, unroll=DMA_UNROLL)
    lax.fori_loop(0, tm, wait, 0, unroll=DMA_UNROLL)
    rt = rt_ref[...]
    y1 = jnp.concatenate([y1_ref[c] for c in range(SUBLANES)], axis=-1)
    y2 = jnp.concatenate([y2_ref[c] for c in range(SUBLANES)], axis=-1)
    y = rt[:, RT_G1:RT_G1 + 1] * y1 + rt[:, RT_G2:RT_G2 + 1] * y2
    o_ref[...] = _deepnorm_ln(x_ref[...], y, mod_ref[0][5:6, :], g_ref[...], b_ref[...])


def _moe_combine_ln(ys, dest1, dest2, rt, x, mod, g, b, *, rows_per_mod, tm):
    t, d = x.shape
    spm = rows_per_mod // tm
    smem = pl.BlockSpec(memory_space=pltpu.SMEM)
    dma = pltpu.SemaphoreType.DMA(())
    return pl.pallas_call(
        _combine_kernel,
        grid=(t // tm,),
        in_specs=[
            smem, smem,
            pl.BlockSpec(memory_space=pl.ANY),
            pl.BlockSpec((tm, LANES), lambda i: (i, 0)),
            pl.BlockSpec((tm, d), lambda i: (i, 0)),
            pl.BlockSpec((1, MOD_ROWS, d), lambda i: (i // spm, 0, 0)),
            _const_spec((1, d)),
            _const_spec((1, d)),
        ],
        out_specs=pl.BlockSpec((tm, d), lambda i: (i, 0)),
        out_shape=jax.ShapeDtypeStruct((t, d), F32),
        scratch_shapes=[pltpu.VMEM((SUBLANES, tm, LANES), F32), pltpu.VMEM((SUBLANES, tm, LANES), F32), dma, dma],
        compiler_params=_cparams(("arbitrary",)),
        name="moe_combine_ln",
    )(dest1, dest2, ys, rt, x, mod, g, b)


def _moe_layout(plan, counts, *, tile):
    n_pairs = plan.shape[1] * 2
    n_tiles = n_pairs // tile + N_EXPERTS
    cnt = counts[0, :N_EXPERTS].astype(jnp.int32)
    padded = (cnt + tile - 1) // tile * tile
    ends = jnp.cumsum(padded)
    starts = ends - padded
    dest1 = starts[plan[2]] + plan[0]
    dest2 = starts[plan[3]] + plan[1]
    n_active = ends[-1:] // tile
    tiles = jnp.minimum(jnp.arange(n_tiles, dtype=jnp.int32), n_active[0] - 1) * tile
    tile_expert = jnp.sum((ends[None, :] <= tiles[:, None]).astype(jnp.int32), axis=1)
    tile_expert = jnp.minimum(tile_expert, N_EXPERTS - 1)
    return dest1, dest2, starts + cnt, ends, tile_expert, n_active.astype(jnp.int32), n_tiles * tile


def _rope_tables(rows_count):
    half = HEAD_DIM // 2
    rows = jnp.repeat(jnp.arange(rows_count, dtype=F32), GRID_W)
    cols = jnp.tile(jnp.arange(GRID_W, dtype=F32), rows_count)
    inv_freq = ROPE_THETA ** (-jnp.arange(0, half, 2, dtype=F32) / half)
    ar = rows[:, None] * inv_freq
    ac = cols[:, None] * inv_freq
    ang = jnp.concatenate([ar, ar, ac, ac], axis=-1)
    sign = jnp.where((jnp.arange(HEAD_DIM) % 32) < 16, -1.0, 1.0).astype(F32)
    cos = jnp.tile(jnp.cos(ang), (1, LANES // HEAD_DIM))
    sin = jnp.tile(jnp.sin(ang) * sign, (1, LANES // HEAD_DIM))
    return cos, sin


def _dup_heads(w, n_heads):
    d = w.shape[0]
    w = w.reshape(d, n_heads, 1, HEAD_DIM)
    return jnp.broadcast_to(w, (d, n_heads, 2, HEAD_DIM)).reshape(d, n_heads * 2 * HEAD_DIM)


def _mod_blocks(mod_rows, d):
    r = mod_rows.shape[0]
    m = mod_rows.reshape(r, 6, d)
    return jnp.concatenate([m, jnp.zeros((r, MOD_ROWS - 6, d), F32)], axis=1)


def kernel(x, c, ctx, c_ctx, w_mod, b_mod, ln_g, ln_b, attn_w_qkv, attn_w_o, attn_sink,
           conv_w_in, conv_w, conv_w_out, ffn_w1, ffn_w3, ffn_w2,
           moe_router, moe_w1, moe_w3, moe_w2):
    bsz, seq, d = x.shape
    n_ctx = ctx.shape[1]
    t = bsz * seq
    q_dim = attn_w_o.shape[1]
    kv_dim = (attn_w_qkv.shape[2] - q_dim) // 2
    n_kv = kv_dim // HEAD_DIM

    pad_rows = (-(bsz + 1)) % 8
    cvec = jnp.concatenate([c, c_ctx[None, :], jnp.zeros((pad_rows, d), F32)], axis=0)
    mod_all = _modulation(cvec, w_mod, b_mod)
    mod0 = _mod_blocks(mod_all[0, :bsz], d)
    modc0 = _mod_blocks(mod_all[0, bsz:bsz + 1], d)
    mod1 = _mod_blocks(mod_all[1, :bsz], d)

    cos, sin = _rope_tables(seq // GRID_W)

    wqkv = attn_w_qkv[0]
    wk = _dup_heads(wqkv[:, q_dim:q_dim + kv_dim], n_kv)
    wv = _dup_heads(wqkv[:, q_dim + kv_dim:], n_kv)
    w_all = jnp.concatenate([wqkv[:, :q_dim], wk, wv], axis=1).astype(BF16)
    w_kv = jnp.concatenate([wk, wv], axis=1).astype(BF16)
    x2d = x.reshape(t, d)
    q, kd, vd = _project(x2d, mod0, w_all, cos, sin, rows_per_mod=seq, tm=512,
                         rope_lanes=q_dim + 2 * kv_dim, q_lanes=q_dim,
                         splits=(q_dim, 2 * kv_dim, 2 * kv_dim), name="qkv_rope")
    (kvc,) = _project(ctx.reshape(bsz * n_ctx, d), modc0, w_kv, cos, sin, rows_per_mod=bsz * n_ctx, tm=512,
                      rope_lanes=0, q_lanes=0, splits=(4 * kv_dim,), name="ctx_kv")
    attn, w1b, w3b, w2b = _attention(q.reshape(bsz, seq, q_dim), kd.reshape(bsz, seq, 2 * kv_dim),
                                     vd.reshape(bsz, seq, 2 * kv_dim), kvc.reshape(bsz, n_ctx, 4 * kv_dim),
                                     attn_sink[0], moe_w1[0], moe_w3[0], moe_w2[0], tq=512)
    ln0 = jnp.concatenate([ln_g[0, 0][None], ln_b[0, 0][None], ln_g[0, 1][None], ln_b[0, 1][None],
                           jnp.zeros((4, d), F32)], axis=0)
    x2, u, bg = _layer0_tail(attn.reshape(t, q_dim), x2d, mod0, mod1, attn_w_o[0].astype(BF16),
                             ffn_w1[0].astype(BF16), ffn_w3[0].astype(BF16), ffn_w2[0].astype(BF16),
                             conv_w_in[0].astype(BF16), ln0, rows_per_mod=seq, tm=512, fc=256)

    cw = jnp.concatenate([conv_w[0], jnp.zeros((8 - CONV_WIDTH, d), F32)], axis=0)
    wr = jnp.concatenate([moe_router[0], jnp.zeros((d, LANES - N_EXPERTS), F32)], axis=1).astype(BF16)
    x3, h4, rt, plan, counts = _conv_out(u.reshape(bsz, seq, d), bg.reshape(bsz, seq, d), cw,
                                         conv_w_out[0].astype(BF16), x2.reshape(bsz, seq, d), mod1,
                                         ln_g[1, 0][None], ln_b[1, 0][None], wr, tm=512)
    rt = rt.reshape(t, LANES)
    dest1, dest2, pad_lo, pad_hi, tile_expert, n_active, n_rows = _moe_layout(plan, counts, tile=MOE_TILE)
    xs = _moe_dispatch(h4, dest1, dest2, pad_lo, pad_hi, n_active, n_rows=n_rows, tm=512, tile=MOE_TILE)
    ys = _moe_experts(xs, tile_expert, n_active, w1b, w3b, w2b, tm=MOE_TILE, fc=512)
    out = _moe_combine_ln(ys, dest1, dest2, rt, x3.reshape(t, d), mod1, ln_g[1, 1][None], ln_b[1, 1][None],
                          rows_per_mod=seq, tm=512)
    return out.reshape(bsz, seq, d)
```
